```python
import math
import jax
import jax.numpy as jnp
from jax import lax
import numpy as np

D_MODEL = 2048
BATCH = 4
SEQ = 4096
DEPTH = 1

CHUNK = 64
Q_BLOCK = 128
SB_HEADS = 8
SB_HEAD_DIM = 128
SB_WIDTH = SB_HEADS * SB_HEAD_DIM
DF_HEADS = 4
DF_HEAD_DIM = 128
DF_V_DIM = 2 * DF_HEAD_DIM
DF_QK_WIDTH = DF_HEADS * 2 * DF_HEAD_DIM
DF_V_WIDTH = DF_HEADS * DF_V_DIM
D_FF = int(math.ceil(8 * D_MODEL / 3 / 256) * 256)
IN_WIDTH = 3 * SB_WIDTH + 2 * DF_QK_WIDTH + DF_V_WIDTH + 2 * D_MODEL
EPS = 1e-6
SUBLN_EPS = 1e-5

kernel_name = "hybrid_stickbreak_diffattn_gated_block"


def rms_norm(x, g, eps=EPS):
    xf = x.astype(jnp.float32)
    y = xf * lax.rsqrt(jnp.mean(xf * xf, axis=-1, keepdims=True) + eps)
    return (y * g.astype(jnp.float32)).astype(x.dtype)


def alibi_slopes(n_heads):
    return jnp.exp2(-8.0 * (jnp.arange(n_heads, dtype=jnp.float32) + 1.0) / n_heads)


def to_blocks(t):
    b, h, s, d = t.shape
    return t.reshape(b, h, s // Q_BLOCK, Q_BLOCK, d).transpose(2, 0, 1, 3, 4)


def from_blocks(t):
    nb, b, h, q, d = t.shape
    return t.transpose(1, 2, 0, 3, 4).reshape(b, h, nb * q, d)


def stick_breaking_attention(q, k, v):
    s_len, d = q.shape[2], q.shape[3]
    scale = 1.0 / math.sqrt(d)
    kpos = jnp.arange(s_len)

    def one_block(args):
        q_blk, i = args
        z = jnp.einsum('bhqd,bhkd->bhqk', q_blk, k).astype(jnp.float32) * scale
        qpos = i * Q_BLOCK + jnp.arange(Q_BLOCK)
        strict = kpos[None, :] < qpos[:, None]
        log_beta = jax.nn.log_sigmoid(z)
        log_one_minus = jnp.where(strict, jax.nn.log_sigmoid(-z), 0.0)
        tail = lax.cumsum(log_one_minus, axis=3, reverse=True) - log_one_minus
        w = jnp.where(strict, jnp.exp(log_beta + tail), 0.0)
        return jnp.einsum('bhqk,bhkd->bhqd', w.astype(v.dtype), v)

    nb = s_len // Q_BLOCK
    out = lax.map(one_block, (to_blocks(q), jnp.arange(nb)))
    return from_blocks(out)


def differential_attention(q1, q2, k1, k2, v, lam):
    n_heads, s_len, d = q1.shape[1], q1.shape[2], q1.shape[3]
    scale = 1.0 / math.sqrt(d)
    slopes = alibi_slopes(n_heads)
    kpos = jnp.arange(s_len)

    def one_block(args):
        q1_blk, q2_blk, i = args
        qpos = i * Q_BLOCK + jnp.arange(Q_BLOCK)
        dist = jnp.abs(qpos[:, None] - kpos[None, :]).astype(jnp.float32)
        allowed = (kpos[None, :] // CHUNK) <= (qpos[:, None] // CHUNK)
        bias = jnp.where(allowed[None], -slopes[:, None, None] * dist[None], -jnp.inf)
        s1 = jnp.einsum('bhqd,bhkd->bhqk', q1_blk, k1).astype(jnp.float32) * scale + bias
        s2 = jnp.einsum('bhqd,bhkd->bhqk', q2_blk, k2).astype(jnp.float32) * scale + bias
        p = jax.nn.softmax(s1, axis=-1) - lam * jax.nn.softmax(s2, axis=-1)
        return jnp.einsum('bhqk,bhkd->bhqd', p.astype(v.dtype), v)

    nb = s_len // Q_BLOCK
    out = lax.map(one_block, (to_blocks(q1), to_blocks(q2), jnp.arange(nb)))
    return from_blocks(out)


def setup_inputs(seed: int = 0) -> dict:
    key = jax.random.key(seed)
    ks = jax.random.split(key, 18)
    f32 = jnp.float32

    def w(k, fan_in, fan_out):
        return jax.random.normal(k, (DEPTH, fan_in, fan_out), f32) * fan_in ** -0.5

    def gain(k, n):
        return 1.0 + 0.01 * jax.random.normal(k, (DEPTH, n), f32)

    return {
        "x": jax.random.normal(ks[0], (BATCH, SEQ, D_MODEL), f32),
        "norm1_g": gain(ks[1], D_MODEL),
        "w_in": w(ks[2], D_MODEL, IN_WIDTH),
        "q_norm_g": gain(ks[3], DF_HEAD_DIM),
        "k_norm_g": gain(ks[4], DF_HEAD_DIM),
        "lambda_q1": 0.1 * jax.random.normal(ks[5], (DEPTH, DF_HEAD_DIM), f32),
        "lambda_k1": 0.1 * jax.random.normal(ks[6], (DEPTH, DF_HEAD_DIM), f32),
        "lambda_q2": 0.1 * jax.random.normal(ks[7], (DEPTH, DF_HEAD_DIM), f32),
        "lambda_k2": 0.1 * jax.random.normal(ks[8], (DEPTH, DF_HEAD_DIM), f32),
        "subln_g": gain(ks[9], DF_V_DIM),
        "w_branch_a": w(ks[10], SB_WIDTH, D_MODEL),
        "w_branch_b": w(ks[11], DF_V_WIDTH, D_MODEL),
        "w_out": w(ks[12], D_MODEL, D_MODEL),
        "norm2_g": gain(ks[13], D_MODEL),
        "w_ffn_gate": w(ks[14], D_MODEL, D_FF),
        "w_ffn_up": w(ks[15], D_MODEL, D_FF),
        "w_ffn_down": w(ks[16], D_FF, D_MODEL),
    }


def reference(x, norm1_g, w_in, q_norm_g, k_norm_g, lambda_q1, lambda_k1, lambda_q2,
              lambda_k2, subln_g, w_branch_a, w_branch_b, w_out, norm2_g,
              w_ffn_gate, w_ffn_up, w_ffn_down):
    b, s, _ = x.shape
    split_points = np.cumsum([SB_WIDTH, SB_WIDTH, SB_WIDTH, DF_QK_WIDTH, DF_QK_WIDTH,
                              DF_V_WIDTH, D_MODEL]).tolist()
    for layer in range(DEPTH):
        lambda_init = 0.8 - 0.6 * math.exp(-0.3 * layer)

        xn = rms_norm(x, norm1_g[layer])
        proj = jnp.einsum('bsd,de->bse', xn, w_in[layer])
        sb_q, sb_k, sb_v, df_q, df_k, df_v, gate_a, gate_b = jnp.split(proj, split_points, axis=-1)

        heads_a = lambda t: t.reshape(b, s, SB_HEADS, SB_HEAD_DIM).transpose(0, 2, 1, 3)
        out_a = stick_breaking_attention(heads_a(sb_q), heads_a(sb_k), heads_a(sb_v))
        out_a = out_a.transpose(0, 2, 1, 3).reshape(b, s, SB_WIDTH)

        dq = df_q.reshape(b, s, DF_HEADS, 2, DF_HEAD_DIM).transpose(3, 0, 2, 1, 4)
        dk = df_k.reshape(b, s, DF_HEADS, 2, DF_HEAD_DIM).transpose(3, 0, 2, 1, 4)
        dq = rms_norm(dq, q_norm_g[layer])
        dk = rms_norm(dk, k_norm_g[layer])
        dv = df_v.reshape(b, s, DF_HEADS, DF_V_DIM).transpose(0, 2, 1, 3)
        lam = (jnp.exp(jnp.sum(lambda_q1[layer] * lambda_k1[layer]).astype(jnp.float32))
               - jnp.exp(jnp.sum(lambda_q2[layer] * lambda_k2[layer]).astype(jnp.float32))
               + lambda_init)
        out_b = differential_attention(dq[0], dq[1], dk[0], dk[1], dv, lam)
        out_b = rms_norm(out_b, subln_g[layer], SUBLN_EPS) * (1.0 - lambda_init)
        out_b = out_b.transpose(0, 2, 1, 3).reshape(b, s, DF_V_WIDTH)

        merged = (jax.nn.sigmoid(gate_a) * jnp.einsum('bse,ed->bsd', out_a, w_branch_a[layer])
                  + jax.nn.sigmoid(gate_b) * jnp.einsum('bse,ed->bsd', out_b, w_branch_b[layer]))
        x = x + jnp.einsum('bsd,de->bse', merged, w_out[layer])

        hn = rms_norm(x, norm2_g[layer])
        hidden = (jax.nn.silu(jnp.einsum('bsd,df->bsf', hn, w_ffn_gate[layer]))
                  * jnp.einsum('bsd,df->bsf', hn, w_ffn_up[layer]))
        x = x + jnp.einsum('bsf,fd->bsd', hidden, w_ffn_down[layer])
    return x
```

```python
import functools
import math

import jax
import jax.numpy as jnp
from jax import lax
from jax.experimental import pallas as pl
from jax.experimental.pallas import tpu as pltpu

F32 = jnp.float32
BF16 = jnp.bfloat16

D_MODEL = 2048
SB_HEADS = 8
HEAD_DIM = 128
DF_HEADS = 4
DF_V_DIM = 256
CHUNK = 64
SB_WIDTH = SB_HEADS * HEAD_DIM
DF_QK_WIDTH = DF_HEADS * 2 * HEAD_DIM
DF_V_WIDTH = DF_HEADS * DF_V_DIM
IN_WIDTH = 3 * SB_WIDTH + 2 * DF_QK_WIDTH + DF_V_WIDTH + 2 * D_MODEL
EPS = 1e-6
SUBLN_EPS = 1e-5
LAMBDA_INIT = 0.8 - 0.6 * math.exp(-0.3 * 0)

OFF_SBQ = 0
OFF_SBK = SB_WIDTH
OFF_SBV = 2 * SB_WIDTH
OFF_DFQ = 3 * SB_WIDTH
OFF_DFK = OFF_DFQ + DF_QK_WIDTH
OFF_DFV = OFF_DFK + DF_QK_WIDTH
OFF_GA = OFF_DFV + DF_V_WIDTH
OFF_GB = OFF_GA + D_MODEL

VMEM_LIMIT = 56 * 1024 * 1024
NEG_BIG = -1e30


def _dot(a, b):
    return jnp.dot(a, b, preferred_element_type=F32)


def _dot_nt(a, b):
    return lax.dot_general(a, b, (((1,), (1,)), ((), ())), preferred_element_type=F32)


PROJ_TN = 1024


def _in_proj_kernel(x_ref, g_ref, w_ref, qg_ref, kg_ref, o_ref, xn_ref):
    j = pl.program_id(1)

    @pl.when(j == 0)
    def _():
        x = x_ref[...]
        ms = jnp.mean(x * x, axis=-1, keepdims=True)
        xn_ref[...] = (x * lax.rsqrt(ms + EPS) * g_ref[...]).astype(BF16)

    acc = _dot(xn_ref[...], w_ref[...])

    is_q = j == OFF_DFQ // PROJ_TN
    is_k = j == OFF_DFK // PROJ_TN
    is_gate = j >= OFF_GA // PROJ_TN

    def qk_norm(gain_ref):
        gain = gain_ref[...]
        for c in range(PROJ_TN // HEAD_DIM):
            a = acc[:, c * HEAD_DIM:(c + 1) * HEAD_DIM]
            ms = jnp.mean(a * a, axis=-1, keepdims=True)
            o_ref[:, c * HEAD_DIM:(c + 1) * HEAD_DIM] = (
                a * lax.rsqrt(ms + EPS) * gain).astype(o_ref.dtype)

    @pl.when(is_q)
    def _():
        qk_norm(qg_ref)

    @pl.when(is_k)
    def _():
        qk_norm(kg_ref)

    @pl.when(is_gate)
    def _():
        o_ref[...] = jax.nn.sigmoid(acc).astype(o_ref.dtype)

    @pl.when(jnp.logical_not(is_q | is_k | is_gate))
    def _():
        o_ref[...] = acc.astype(o_ref.dtype)


def _in_proj(x2d, g1, w_in, qg, kg, tm=1024):
    t = x2d.shape[0]
    grid = (t // tm, IN_WIDTH // PROJ_TN)
    return pl.pallas_call(
        _in_proj_kernel,
        out_shape=jax.ShapeDtypeStruct((t, IN_WIDTH), BF16),
        grid=grid,
        in_specs=[
            pl.BlockSpec((tm, D_MODEL), lambda i, j: (i, 0)),
            pl.BlockSpec((1, D_MODEL), lambda i, j: (0, 0)),
            pl.BlockSpec((D_MODEL, PROJ_TN), lambda i, j: (0, j)),
            pl.BlockSpec((1, HEAD_DIM), lambda i, j: (0, 0)),
            pl.BlockSpec((1, HEAD_DIM), lambda i, j: (0, 0)),
        ],
        out_specs=pl.BlockSpec((tm, PROJ_TN), lambda i, j: (i, j)),
        scratch_shapes=[pltpu.VMEM((tm, D_MODEL), BF16)],
        compiler_params=pltpu.CompilerParams(
            dimension_semantics=("arbitrary", "arbitrary"),
            vmem_limit_bytes=VMEM_LIMIT),
        name="in_proj",
    )(x2d, g1, w_in, qg, kg)


SB_T = 256


def _split_hi_lo(a):
    hi = a.astype(BF16)
    lo = (a - hi.astype(F32)).astype(BF16)
    return hi, lo


def _sb_attn_kernel(q_ref, k_ref, v_ref, o_ref):
    qi = pl.program_id(2)
    t = SB_T
    scale = 1.0 / math.sqrt(HEAD_DIM)
    q = q_ref[0]

    row = lax.broadcasted_iota(jnp.int32, (t, t), 0)
    col = lax.broadcasted_iota(jnp.int32, (t, t), 1)
    u_incl = (row >= col).astype(BF16)
    strict = col < row

    def neg_softplus(z):
        return -(jnp.maximum(z, 0.0) + jnp.log(1.0 + jnp.exp(-jnp.abs(z))))

    def block(kb, carry, masked):
        start = pl.multiple_of(kb * t, t)
        kblk = k_ref[0, pl.ds(start, t), :]
        vblk = v_ref[0, pl.ds(start, t), :]
        z = _dot_nt(q, kblk) * scale
        lom = neg_softplus(z)
        if masked:
            lom = jnp.where(strict, lom, 0.0)
        hi, lo = _split_hi_lo(lom)
        cum = _dot(hi, u_incl) + _dot(lo, u_incl)
        e = z + cum
        if carry is not None:
            e = e + carry
        w = jnp.exp(e)
        if masked:
            w = jnp.where(strict, w, 0.0)
        pv = _dot(w.astype(BF16), vblk)
        return pv, cum[:, 0:1]

    acc, carry = block(qi, None, True)

    def body(n, st):
        acc, carry = st
        pv, rs = block(qi - 1 - n, carry, False)
        return acc + pv, carry + rs

    acc, carry = lax.fori_loop(0, qi, body, (acc, carry))
    o_ref[0] = acc.astype(o_ref.dtype)


def _sb_attn(proj3d):
    b, s, _ = proj3d.shape
    t = SB_T
    grid = (b, SB_HEADS, s // t)
    cq, ck, cv = OFF_SBQ // HEAD_DIM, OFF_SBK // HEAD_DIM, OFF_SBV // HEAD_DIM
    return pl.pallas_call(
        _sb_attn_kernel,
        out_shape=jax.ShapeDtypeStruct((b, s, SB_WIDTH), BF16),
        grid=grid,
        in_specs=[
            pl.BlockSpec((1, t, HEAD_DIM), lambda bi, h, qi: (bi, qi, cq + h)),
            pl.BlockSpec((1, s, HEAD_DIM), lambda bi, h, qi: (bi, 0, ck + h)),
            pl.BlockSpec((1, s, HEAD_DIM), lambda bi, h, qi: (bi, 0, cv + h)),
        ],
        out_specs=pl.BlockSpec((1, t, HEAD_DIM), lambda bi, h, qi: (bi, qi, h)),
        compiler_params=pltpu.CompilerParams(
            dimension_semantics=("arbitrary", "arbitrary", "arbitrary"),
            vmem_limit_bytes=VMEM_LIMIT),
        name="sb_attn",
    )(proj3d, proj3d, proj3d)


DF_T = 256


def _df_attn_kernel(lam_ref, q1_ref, q2_ref, k1_ref, k2_ref, v_ref, sg_ref, o_ref):
    h = pl.program_id(1)
    qi = pl.program_id(2)
    t = DF_T
    scale = 1.0 / math.sqrt(HEAD_DIM)
    slope = jnp.exp2(-8.0 * (h.astype(F32) + 1.0) / DF_HEADS)
    lam = lam_ref[0]

    row = lax.broadcasted_iota(jnp.int32, (t, t), 0)
    col = lax.broadcasted_iota(jnp.int32, (t, t), 1)
    rel = (row - col).astype(F32)
    allowed = (col // CHUNK) <= (row // CHUNK)
    bias_off = -slope * rel
    bias_diag = jnp.where(allowed, -slope * jnp.abs(rel), NEG_BIG)

    q1 = q1_ref[0]
    q2 = q2_ref[0]

    def scores(kb, bias, shift):
        start = pl.multiple_of(kb * t, t)
        s1 = _dot_nt(q1, k1_ref[0, pl.ds(start, t), :]) * scale + bias
        s2 = _dot_nt(q2, k2_ref[0, pl.ds(start, t), :]) * scale + bias
        vblk = v_ref[0, pl.ds(start, t), :]
        return s1, s2, vblk

    def update(st, s, vblk, shift):
        m, l, acc = st
        m_new = jnp.maximum(m, jnp.max(s, axis=-1, keepdims=True) + shift)
        alpha = jnp.exp(m - m_new)
        p = jnp.exp(s - (m_new - shift))
        l = alpha * l + jnp.sum(p, axis=-1, keepdims=True)
        acc = alpha * acc + _dot(p.astype(BF16), vblk)
        return m_new, l, acc

    def init():
        return (jnp.full((t, 1), NEG_BIG, F32), jnp.zeros((t, 1), F32),
                jnp.zeros((t, DF_V_DIM), F32))

    def body(kb, st):
        st1, st2 = st
        shift = -slope * ((qi - kb) * t).astype(F32)
        s1, s2, vblk = scores(kb, bias_off, shift)
        return update(st1, s1, vblk, shift), update(st2, s2, vblk, shift)

    st1, st2 = lax.fori_loop(0, qi, body, (init(), init()))
    s1, s2, vblk = scores(qi, bias_diag, 0.0)
    _, l1, a1 = update(st1, s1, vblk, 0.0)
    _, l2, a2 = update(st2, s2, vblk, 0.0)

    out = a1 / l1 - lam * (a2 / l2)
    ms = jnp.mean(out * out, axis=-1, keepdims=True)
    out = out * lax.rsqrt(ms + SUBLN_EPS) * sg_ref[...] * (1.0 - LAMBDA_INIT)
    o_ref[0] = out.astype(o_ref.dtype)


def _df_attn(proj3d, lam, subln_g):
    b, s, _ = proj3d.shape
    t = DF_T
    grid = (b, DF_HEADS, s // t)
    cq, ck, cv = OFF_DFQ // HEAD_DIM, OFF_DFK // HEAD_DIM, OFF_DFV // DF_V_DIM
    return pl.pallas_call(
        _df_attn_kernel,
        out_shape=jax.ShapeDtypeStruct((b, s, DF_V_WIDTH), BF16),
        grid=grid,
        in_specs=[
            pl.BlockSpec(memory_space=pltpu.SMEM),
            pl.BlockSpec((1, t, HEAD_DIM), lambda bi, h, qi: (bi, qi, cq + 2 * h)),
            pl.BlockSpec((1, t, HEAD_DIM), lambda bi, h, qi: (bi, qi, cq + 2 * h + 1)),
            pl.BlockSpec((1, s, HEAD_DIM), lambda bi, h, qi: (bi, 0, ck + 2 * h)),
            pl.BlockSpec((1, s, HEAD_DIM), lambda bi, h, qi: (bi, 0, ck + 2 * h + 1)),
            pl.BlockSpec((1, s, DF_V_DIM), lambda bi, h, qi: (bi, 0, cv + h)),
            pl.BlockSpec((1, DF_V_DIM), lambda bi, h, qi: (0, 0)),
        ],
        out_specs=pl.BlockSpec((1, t, DF_V_DIM), lambda bi, h, qi: (bi, qi, h)),
        compiler_params=pltpu.CompilerParams(
            dimension_semantics=("arbitrary", "arbitrary", "arbitrary"),
            vmem_limit_bytes=VMEM_LIMIT),
        name="df_attn",
    )(lam, proj3d, proj3d, proj3d, proj3d, proj3d, subln_g)


def _merge_kernel(a_ref, wa_ref, b_ref, wb_ref, ga_ref, gb_ref, o_ref):
    pa = _dot(a_ref[...], wa_ref[...])
    pb = _dot(b_ref[...], wb_ref[...])
    o_ref[...] = (ga_ref[...].astype(F32) * pa + gb_ref[...].astype(F32) * pb).astype(o_ref.dtype)


def _merge(out_a, wa, out_b, wb, proj2d, tm=1024, tn=1024):
    t = out_a.shape[0]
    grid = (t // tm, D_MODEL // tn)
    ca, cb = OFF_GA // tn, OFF_GB // tn
    return pl.pallas_call(
        _merge_kernel,
        out_shape=jax.ShapeDtypeStruct((t, D_MODEL), BF16),
        grid=grid,
        in_specs=[
            pl.BlockSpec((tm, SB_WIDTH), lambda i, j: (i, 0)),
            pl.BlockSpec((SB_WIDTH, tn), lambda i, j: (0, j)),
            pl.BlockSpec((tm, DF_V_WIDTH), lambda i, j: (i, 0)),
            pl.BlockSpec((DF_V_WIDTH, tn), lambda i, j: (0, j)),
            pl.BlockSpec((tm, tn), lambda i, j: (i, ca + j)),
            pl.BlockSpec((tm, tn), lambda i, j: (i, cb + j)),
        ],
        out_specs=pl.BlockSpec((tm, tn), lambda i, j: (i, j)),
        compiler_params=pltpu.CompilerParams(
            dimension_semantics=("arbitrary", "arbitrary"),
            vmem_limit_bytes=VMEM_LIMIT),
        name="merge",
    )(out_a, wa, out_b, wb, proj2d, proj2d)


def _out_proj_kernel(m_ref, w_ref, x_ref, g_ref, h_ref, hn_ref):
    h = x_ref[...] + _dot(m_ref[...], w_ref[...])
    h_ref[...] = h
    ms = jnp.mean(h * h, axis=-1, keepdims=True)
    hn_ref[...] = (h * lax.rsqrt(ms + EPS) * g_ref[...]).astype(hn_ref.dtype)


def _out_proj(merged, w_out, x2d, g2, tm=512):
    t = merged.shape[0]
    grid = (t // tm,)
    return pl.pallas_call(
        _out_proj_kernel,
        out_shape=(jax.ShapeDtypeStruct((t, D_MODEL), F32),
                   jax.ShapeDtypeStruct((t, D_MODEL), BF16)),
        grid=grid,
        in_specs=[
            pl.BlockSpec((tm, D_MODEL), lambda i: (i, 0)),
            pl.BlockSpec((D_MODEL, D_MODEL), lambda i: (0, 0)),
            pl.BlockSpec((tm, D_MODEL), lambda i: (i, 0)),
            pl.BlockSpec((1, D_MODEL), lambda i: (0, 0)),
        ],
        out_specs=(pl.BlockSpec((tm, D_MODEL), lambda i: (i, 0)),
                   pl.BlockSpec((tm, D_MODEL), lambda i: (i, 0))),
        compiler_params=pltpu.CompilerParams(
            dimension_semantics=("arbitrary",),
            vmem_limit_bytes=VMEM_LIMIT),
        name="out_proj",
    )(merged, w_out, x2d, g2)


def _ffn_kernel(hn_ref, h_ref, wg_ref, wu_ref, wd_ref, o_ref):
    j = pl.program_id(1)

    @pl.when(j == 0)
    def _():
        o_ref[...] = h_ref[...]

    hn = hn_ref[...]
    g = _dot(hn, wg_ref[...])
    u = _dot(hn, wu_ref[...])
    hid = (g * jax.nn.sigmoid(g) * u).astype(BF16)
    o_ref[...] += _dot(hid, wd_ref[...])


def _ffn(hn, h, wg, wu, wd, tm=512, tf=512):
    t = hn.shape[0]
    d_ff = wg.shape[1]
    grid = (t // tm, d_ff // tf)
    return pl.pallas_call(
        _ffn_kernel,
        out_shape=jax.ShapeDtypeStruct((t, D_MODEL), F32),
        grid=grid,
        in_specs=[
            pl.BlockSpec((tm, D_MODEL), lambda i, j: (i, 0)),
            pl.BlockSpec((tm, D_MODEL), lambda i, j: (i, 0)),
            pl.BlockSpec((D_MODEL, tf), lambda i, j: (0, j)),
            pl.BlockSpec((D_MODEL, tf), lambda i, j: (0, j)),
            pl.BlockSpec((tf, D_MODEL), lambda i, j: (j, 0)),
        ],
        out_specs=pl.BlockSpec((tm, D_MODEL), lambda i, j: (i, 0)),
        compiler_params=pltpu.CompilerParams(
            dimension_semantics=("arbitrary", "arbitrary"),
            vmem_limit_bytes=VMEM_LIMIT),
        name="ffn",
    )(hn, h, wg, wu, wd)


def kernel(x, norm1_g, w_in, q_norm_g, k_norm_g, lambda_q1, lambda_k1, lambda_q2, lambda_k2,
           subln_g, w_branch_a, w_branch_b, w_out, norm2_g, w_ffn_gate, w_ffn_up, w_ffn_down):
    b, s, d = x.shape
    t = b * s
    layer = 0
    x2d = x.reshape(t, d)

    lam = (jnp.exp(jnp.sum(lambda_q1[layer] * lambda_k1[layer]))
           - jnp.exp(jnp.sum(lambda_q2[layer] * lambda_k2[layer]))
           + LAMBDA_INIT).astype(F32).reshape(1)

    proj = _in_proj(x2d, norm1_g[layer].reshape(1, d), w_in[layer].astype(BF16),
                    q_norm_g[layer].reshape(1, HEAD_DIM), k_norm_g[layer].reshape(1, HEAD_DIM))
    proj3d = proj.reshape(b, s, IN_WIDTH)

    out_a = _sb_attn(proj3d).reshape(t, SB_WIDTH)
    out_b = _df_attn(proj3d, lam, subln_g[layer].reshape(1, DF_V_DIM)).reshape(t, DF_V_WIDTH)

    merged = _merge(out_a, w_branch_a[layer].astype(BF16), out_b, w_branch_b[layer].astype(BF16), proj)
    h, hn = _out_proj(merged, w_out[layer].astype(BF16), x2d, norm2_g[layer].reshape(1, d))
    out = _ffn(hn, h, w_ffn_gate[layer].astype(BF16), w_ffn_up[layer].astype(BF16),
               w_ffn_down[layer].astype(BF16))
    return out.reshape(b, s, d)
```

```python
import math

import jax
import jax.numpy as jnp
from jax import lax
from jax.experimental import pallas as pl
from jax.experimental.pallas import tpu as pltpu

F32 = jnp.float32
BF16 = jnp.bfloat16

D_MODEL = 2048
SB_HEADS = 8
HEAD_DIM = 128
DF_HEADS = 4
DF_V_DIM = 256
CHUNK = 64
SB_WIDTH = SB_HEADS * HEAD_DIM
DF_QK_WIDTH = DF_HEADS * 2 * HEAD_DIM
DF_V_WIDTH = DF_HEADS * DF_V_DIM
IN_WIDTH = 3 * SB_WIDTH + 2 * DF_QK_WIDTH + DF_V_WIDTH + 2 * D_MODEL
EPS = 1e-6
SUBLN_EPS = 1e-5
LAMBDA_INIT = 0.8 - 0.6 * math.exp(-0.3 * 0)
LOG2E = math.log2(math.e)
Q_SCALE = LOG2E / math.sqrt(HEAD_DIM)

OFF_SBQ = 0
OFF_SBK = SB_WIDTH
OFF_SBV = 2 * SB_WIDTH
OFF_DFQ = 3 * SB_WIDTH
OFF_DFK = OFF_DFQ + DF_QK_WIDTH
OFF_DFV = OFF_DFK + DF_QK_WIDTH
OFF_GA = OFF_DFV + DF_V_WIDTH
OFF_GB = OFF_GA + D_MODEL

VMEM_LIMIT = 56 * 1024 * 1024
NEG_BIG = -1e30


def _dot(a, b):
    return jnp.dot(a, b, preferred_element_type=F32)


def _dot_nt(a, b):
    return lax.dot_general(a, b, (((1,), (1,)), ((), ())), preferred_element_type=F32)


PROJ_TN = 1024


def _in_proj_kernel(x_ref, g_ref, w_ref, qg_ref, kg_ref, o_ref, xn_ref):
    j = pl.program_id(1)

    @pl.when(j == 0)
    def _():
        x = x_ref[...]
        ms = jnp.mean(x * x, axis=-1, keepdims=True)
        xn_ref[...] = (x * lax.rsqrt(ms + EPS) * g_ref[...]).astype(BF16)

    acc = _dot(xn_ref[...], w_ref[...])

    is_sbq = j == OFF_SBQ // PROJ_TN
    is_q = j == OFF_DFQ // PROJ_TN
    is_k = j == OFF_DFK // PROJ_TN
    is_gate = j >= OFF_GA // PROJ_TN

    def qk_norm(gain_ref, post_scale):
        gain = gain_ref[...] * post_scale
        for c in range(PROJ_TN // HEAD_DIM):
            a = acc[:, c * HEAD_DIM:(c + 1) * HEAD_DIM]
            ms = jnp.mean(a * a, axis=-1, keepdims=True)
            o_ref[:, c * HEAD_DIM:(c + 1) * HEAD_DIM] = (
                a * lax.rsqrt(ms + EPS) * gain).astype(o_ref.dtype)

    @pl.when(is_sbq)
    def _():
        o_ref[...] = (acc * Q_SCALE).astype(o_ref.dtype)

    @pl.when(is_q)
    def _():
        qk_norm(qg_ref, Q_SCALE)

    @pl.when(is_k)
    def _():
        qk_norm(kg_ref, 1.0)

    @pl.when(is_gate)
    def _():
        o_ref[...] = jax.nn.sigmoid(acc).astype(o_ref.dtype)

    @pl.when(jnp.logical_not(is_sbq | is_q | is_k | is_gate))
    def _():
        o_ref[...] = acc.astype(o_ref.dtype)


def _in_proj(x2d, g1, w_in, qg, kg, tm=1024):
    t = x2d.shape[0]
    grid = (t // tm, IN_WIDTH // PROJ_TN)
    return pl.pallas_call(
        _in_proj_kernel,
        out_shape=jax.ShapeDtypeStruct((t, IN_WIDTH), BF16),
        grid=grid,
        in_specs=[
            pl.BlockSpec((tm, D_MODEL), lambda i, j: (i, 0)),
            pl.BlockSpec((1, D_MODEL), lambda i, j: (0, 0)),
            pl.BlockSpec((D_MODEL, PROJ_TN), lambda i, j: (0, j)),
            pl.BlockSpec((1, HEAD_DIM), lambda i, j: (0, 0)),
            pl.BlockSpec((1, HEAD_DIM), lambda i, j: (0, 0)),
        ],
        out_specs=pl.BlockSpec((tm, PROJ_TN), lambda i, j: (i, j)),
        scratch_shapes=[pltpu.VMEM((tm, D_MODEL), BF16)],
        compiler_params=pltpu.CompilerParams(
            dimension_semantics=("arbitrary", "arbitrary"),
            vmem_limit_bytes=VMEM_LIMIT),
        name="in_proj",
    )(x2d, g1, w_in, qg, kg)


SB_T = 256
SB_G = 4


def _softplus2(a):
    return jnp.maximum(a, 0.0) + jnp.log2(1.0 + jnp.exp2(jnp.minimum(a, -a)))


def _split_hi_lo(a):
    hi = a.astype(BF16)
    lo = (a - hi.astype(F32)).astype(BF16)
    return hi, lo


def _eye(n):
    return (lax.broadcasted_iota(jnp.int32, (n, n), 0)
            == lax.broadcasted_iota(jnp.int32, (n, n), 1)).astype(BF16)


def _fill_vt(v_ref, vt_ref, t):
    eye = _eye(HEAD_DIM)
    n_groups = v_ref.shape[2] // HEAD_DIM

    def body(c, _):
        start = pl.multiple_of(c * t, t)
        for j in range(n_groups):
            lanes = slice(j * HEAD_DIM, (j + 1) * HEAD_DIM)
            vt_ref[c, lanes, :] = _dot_nt(eye, v_ref[0, pl.ds(start, t), lanes]).astype(BF16)
        return 0

    lax.fori_loop(0, v_ref.shape[1] // t, body, 0)


def _sb_attn_kernel(q_ref, k_ref, v_ref, o_ref, vt_ref, a_ref, hl_ref, e_ref):
    qi = pl.program_id(2)
    t = SB_T
    heads = range(SB_G)

    @pl.when(qi == 0)
    def _():
        _fill_vt(v_ref, vt_ref, t)

    r2 = lax.broadcasted_iota(jnp.int32, (t, 2 * t), 0)
    c2 = lax.broadcasted_iota(jnp.int32, (t, 2 * t), 1)
    l2 = ((c2 & (t - 1)) >= r2).astype(BF16)
    lanes = [slice(g * HEAD_DIM, (g + 1) * HEAD_DIM) for g in heads]

    def block_start(j):
        return pl.multiple_of(jnp.maximum(qi - j, 0) * t, t)

    def strict_mask():
        key = lax.broadcasted_iota(jnp.int32, (t, t), 0)
        qry = lax.broadcasted_iota(jnp.int32, (t, t), 1)
        return key < qry

    def s1_matmul(j):
        start = block_start(j)
        return [_dot_nt(k_ref[0, pl.ds(start, t), lanes[g]], q_ref[0, :, lanes[g]]) for g in heads]

    def s1_finish(a, slot, masked):
        for g in heads:
            sp = _softplus2(a[g])
            if masked:
                sp = jnp.where(strict_mask(), sp, 0.0)
            hi, lo = _split_hi_lo(sp)
            a_ref[slot, g] = a[g]
            hl_ref[slot, g, 0:t, :] = hi
            hl_ref[slot, g, t:2 * t, :] = lo

    def s2_matmul(slot):
        return [_dot(l2, hl_ref[slot, g]) for g in heads]

    def s2_finish(cum, slot, masked):
        for g in heads:
            e0 = a_ref[slot, g] - cum[g]
            if masked:
                e0 = jnp.where(strict_mask(), e0, NEG_BIG)
            e_ref[slot, g] = e0
        return tuple(c[0:1, :] for c in cum)

    def s3_weights(slot, carry):
        return [jnp.exp2(e_ref[slot, g] - carry[g]).astype(BF16) for g in heads]

    def s3_matmul(j, w):
        kb = jnp.maximum(qi - j, 0)
        return [_dot(vt_ref[kb, lanes[g], :], w[g]) for g in heads]

    s1_finish(s1_matmul(0), 0, True)
    a = s1_matmul(1)
    cum = s2_matmul(0)
    s1_finish(a, 1, False)
    sums = s2_finish(cum, 0, True)
    init = (tuple(jnp.zeros((HEAD_DIM, t), F32) for _ in heads),
            tuple(jnp.zeros((1, t), F32) for _ in heads), sums)

    def step(i, slot, st):
        acc, carry, sums = st
        w = s3_weights(slot, carry)
        a = s1_matmul(i)
        cum = s2_matmul(1 - slot)
        pv = s3_matmul(i - 2, w)
        s1_finish(a, slot, False)
        new_sums = s2_finish(cum, 1 - slot, False)
        return (tuple(acc[g] + pv[g] for g in heads),
                tuple(carry[g] + sums[g] for g in heads), new_sums)

    def body(k, st):
        return step(2 * k + 3, 1, step(2 * k + 2, 0, st))

    pairs = qi // 2
    acc, carry, sums = lax.fori_loop(0, pairs, body, init)

    w = s3_weights(0, carry)
    cum = s2_matmul(1)
    pv = s3_matmul(2 * pairs, w)
    s2_finish(cum, 1, False)
    acc = tuple(acc[g] + pv[g] for g in heads)
    carry = tuple(carry[g] + sums[g] for g in heads)

    pv = s3_matmul(2 * pairs + 1, s3_weights(1, carry))
    last_is_real = (qi & 1) == 1
    for g in heads:
        out = acc[g] + jnp.where(last_is_real, pv[g], 0.0)
        o_ref[0, :, lanes[g]] = out.T.astype(o_ref.dtype)


def _sb_attn(proj3d):
    b, s, _ = proj3d.shape
    t = SB_T
    w = SB_G * HEAD_DIM
    grid = (b, SB_HEADS // SB_G, s // t)
    cq, ck, cv = OFF_SBQ // w, OFF_SBK // w, OFF_SBV // w
    single = pl.Buffered(1)
    return pl.pallas_call(
        _sb_attn_kernel,
        out_shape=jax.ShapeDtypeStruct((b, s, SB_WIDTH), BF16),
        grid=grid,
        in_specs=[
            pl.BlockSpec((1, t, w), lambda bi, h, qi: (bi, qi, cq + h)),
            pl.BlockSpec((1, s, w), lambda bi, h, qi: (bi, 0, ck + h), pipeline_mode=single),
            pl.BlockSpec((1, s, w), lambda bi, h, qi: (bi, 0, cv + h), pipeline_mode=single),
        ],
        out_specs=pl.BlockSpec((1, t, w), lambda bi, h, qi: (bi, qi, h)),
        scratch_shapes=[pltpu.VMEM((s // t, w, t), BF16),
                        pltpu.VMEM((2, SB_G, t, t), F32),
                        pltpu.VMEM((2, SB_G, 2 * t, t), BF16),
                        pltpu.VMEM((2, SB_G, t, t), F32)],
        compiler_params=pltpu.CompilerParams(
            dimension_semantics=("arbitrary", "arbitrary", "arbitrary"),
            vmem_limit_bytes=VMEM_LIMIT),
        name="sb_attn",
    )(proj3d, proj3d, proj3d)


DF_T = 256
DF_SLOPES = [2.0 ** (-8.0 * (h + 1) / DF_HEADS) for h in range(DF_HEADS)]


def _df_attn_kernel(lamp_ref, q_ref, k_ref, v_ref, sg_ref, o_ref, bias_ref, acc_ref, vt_ref):
    bi = pl.program_id(0)
    qi = pl.program_id(1)
    t = DF_T

    @pl.when(qi == 0)
    def _():
        _fill_vt(v_ref, vt_ref, t)

    @pl.when((bi == 0) & (qi == 0))
    def _():
        key = lax.broadcasted_iota(jnp.int32, (t, t), 0)
        qry = lax.broadcasted_iota(jnp.int32, (t, t), 1)
        rel = (qry - key).astype(F32)
        allowed = (key // CHUNK) <= (qry // CHUNK)
        for h in range(DF_HEADS):
            sl = DF_SLOPES[h] * LOG2E
            bias_ref[h, 0] = -sl * rel
            bias_ref[h, 1] = jnp.where(allowed, -sl * jnp.abs(rel), NEG_BIG)

    chains = [(h, c) for h in range(DF_HEADS) for c in range(2)]

    def update(st, kb, which, dist):
        start = pl.multiple_of(kb * t, t)
        a = []
        for h, c in chains:
            lanes = slice((2 * h + c) * HEAD_DIM, (2 * h + c + 1) * HEAD_DIM)
            a.append(_dot_nt(k_ref[0, pl.ds(start, t), lanes], q_ref[0, :, lanes]))
        new_st, p, alpha = [], [], []
        for i, (h, c) in enumerate(chains):
            m, l = st[i]
            shift = -(DF_SLOPES[h] * LOG2E) * dist
            s = a[i] + bias_ref[h, which]
            m_new = jnp.maximum(m, jnp.max(s, axis=0, keepdims=True) + shift)
            al = jnp.exp2(m - m_new)
            pi = jnp.exp2(s - (m_new - shift))
            new_st.append((m_new, al * l + jnp.sum(pi, axis=0, keepdims=True)))
            p.append(pi.astype(BF16))
            alpha.append(al)
        for i, (h, c) in enumerate(chains):
            vt = vt_ref[kb, h * DF_V_DIM:(h + 1) * DF_V_DIM, :]
            acc_ref[h, c] = alpha[i] * acc_ref[h, c] + _dot(vt, p[i])
        return tuple(new_st)

    acc_ref[...] = jnp.zeros_like(acc_ref)
    init = tuple((jnp.full((1, t), NEG_BIG, F32), jnp.zeros((1, t), F32)) for _ in chains)

    def body(kb, st):
        return update(st, kb, 0, ((qi - kb) * t).astype(F32))

    st = lax.fori_loop(0, qi, body, init)
    st = update(st, qi, 1, 0.0)

    lp = lamp_ref[...]
    lam = (jnp.exp(jnp.sum(lp[0:1] * lp[1:2], axis=-1, keepdims=True))
           - jnp.exp(jnp.sum(lp[2:3] * lp[3:4], axis=-1, keepdims=True)) + LAMBDA_INIT)
    for h in range(DF_HEADS):
        l1, l2 = st[2 * h][1], st[2 * h + 1][1]
        out_t = acc_ref[h, 0] / l1 - lam * (acc_ref[h, 1] / l2)
        ms = jnp.mean(out_t * out_t, axis=0, keepdims=True)
        out = (out_t * lax.rsqrt(ms + SUBLN_EPS)).T * (sg_ref[...] * (1.0 - LAMBDA_INIT))
        o_ref[0, :, h * DF_V_DIM:(h + 1) * DF_V_DIM] = out.astype(o_ref.dtype)


def _df_attn(proj3d, lam_params, subln_g):
    b, s, _ = proj3d.shape
    t = DF_T
    grid = (b, s // t)
    cq, ck, cv = OFF_DFQ // DF_QK_WIDTH, OFF_DFK // DF_QK_WIDTH, OFF_DFV // DF_V_WIDTH
    single = pl.Buffered(1)
    return pl.pallas_call(
        _df_attn_kernel,
        out_shape=jax.ShapeDtypeStruct((b, s, DF_V_WIDTH), BF16),
        grid=grid,
        in_specs=[
            pl.BlockSpec((4, HEAD_DIM), lambda bi, qi: (0, 0)),
            pl.BlockSpec((1, t, DF_QK_WIDTH), lambda bi, qi: (bi, qi, cq)),
            pl.BlockSpec((1, s, DF_QK_WIDTH), lambda bi, qi: (bi, 0, ck), pipeline_mode=single),
            pl.BlockSpec((1, s, DF_V_WIDTH), lambda bi, qi: (bi, 0, cv), pipeline_mode=single),
            pl.BlockSpec((1, DF_V_DIM), lambda bi, qi: (0, 0)),
        ],
        out_specs=pl.BlockSpec((1, t, DF_V_WIDTH), lambda bi, qi: (bi, qi, 0)),
        scratch_shapes=[pltpu.VMEM((DF_HEADS, 2, t, t), F32),
                        pltpu.VMEM((DF_HEADS, 2, DF_V_DIM, t), F32),
                        pltpu.VMEM((s // t, DF_V_WIDTH, t), BF16)],
        compiler_params=pltpu.CompilerParams(
            dimension_semantics=("arbitrary", "arbitrary"),
            vmem_limit_bytes=VMEM_LIMIT),
        name="df_attn",
    )(lam_params, proj3d, proj3d, proj3d, subln_g)


def _merge_kernel(a_ref, wa_ref, b_ref, wb_ref, ga_ref, gb_ref, o_ref):
    pa = _dot(a_ref[...], wa_ref[...])
    pb = _dot(b_ref[...], wb_ref[...])
    o_ref[...] = (ga_ref[...].astype(F32) * pa + gb_ref[...].astype(F32) * pb).astype(o_ref.dtype)


def _merge(out_a, wa, out_b, wb, proj2d, tm=1024, tn=1024):
    t = out_a.shape[0]
    grid = (t // tm, D_MODEL // tn)
    ca, cb = OFF_GA // tn, OFF_GB // tn
    return pl.pallas_call(
        _merge_kernel,
        out_shape=jax.ShapeDtypeStruct((t, D_MODEL), BF16),
        grid=grid,
        in_specs=[
            pl.BlockSpec((tm, SB_WIDTH), lambda i, j: (i, 0)),
            pl.BlockSpec((SB_WIDTH, tn), lambda i, j: (0, j)),
            pl.BlockSpec((tm, DF_V_WIDTH), lambda i, j: (i, 0)),
            pl.BlockSpec((DF_V_WIDTH, tn), lambda i, j: (0, j)),
            pl.BlockSpec((tm, tn), lambda i, j: (i, ca + j)),
            pl.BlockSpec((tm, tn), lambda i, j: (i, cb + j)),
        ],
        out_specs=pl.BlockSpec((tm, tn), lambda i, j: (i, j)),
        compiler_params=pltpu.CompilerParams(
            dimension_semantics=("arbitrary", "arbitrary"),
            vmem_limit_bytes=VMEM_LIMIT),
        name="merge",
    )(out_a, wa, out_b, wb, proj2d, proj2d)


def _out_proj_kernel(m_ref, w_ref, x_ref, g_ref, h_ref, hn_ref):
    h = x_ref[...] + _dot(m_ref[...], w_ref[...])
    h_ref[...] = h
    ms = jnp.mean(h * h, axis=-1, keepdims=True)
    hn_ref[...] = (h * lax.rsqrt(ms + EPS) * g_ref[...]).astype(hn_ref.dtype)


def _out_proj(merged, w_out, x2d, g2, tm=512):
    t = merged.shape[0]
    grid = (t // tm,)
    return pl.pallas_call(
        _out_proj_kernel,
        out_shape=(jax.ShapeDtypeStruct((t, D_MODEL), F32),
                   jax.ShapeDtypeStruct((t, D_MODEL), BF16)),
        grid=grid,
        in_specs=[
            pl.BlockSpec((tm, D_MODEL), lambda i: (i, 0)),
            pl.BlockSpec((D_MODEL, D_MODEL), lambda i: (0, 0)),
            pl.BlockSpec((tm, D_MODEL), lambda i: (i, 0)),
            pl.BlockSpec((1, D_MODEL), lambda i: (0, 0)),
        ],
        out_specs=(pl.BlockSpec((tm, D_MODEL), lambda i: (i, 0)),
                   pl.BlockSpec((tm, D_MODEL), lambda i: (i, 0))),
        compiler_params=pltpu.CompilerParams(
            dimension_semantics=("arbitrary",),
            vmem_limit_bytes=VMEM_LIMIT),
        name="out_proj",
    )(merged, w_out, x2d, g2)


def _ffn_kernel(hn_ref, h_ref, wg_ref, wu_ref, wd_ref, o_ref):
    j = pl.program_id(1)

    @pl.when(j == 0)
    def _():
        o_ref[...] = h_ref[...]

    hn = hn_ref[...]
    g = _dot(hn, wg_ref[...])
    u = _dot(hn, wu_ref[...])
    hid = (g * jax.nn.sigmoid(g) * u).astype(BF16)
    o_ref[...] += _dot(hid, wd_ref[...])


def _ffn(hn, h, wg, wu, wd, tm=512, tf=512):
    t = hn.shape[0]
    d_ff = wg.shape[1]
    grid = (t // tm, d_ff // tf)
    return pl.pallas_call(
        _ffn_kernel,
        out_shape=jax.ShapeDtypeStruct((t, D_MODEL), F32),
        grid=grid,
        in_specs=[
            pl.BlockSpec((tm, D_MODEL), lambda i, j: (i, 0)),
            pl.BlockSpec((tm, D_MODEL), lambda i, j: (i, 0)),
            pl.BlockSpec((D_MODEL, tf), lambda i, j: (0, j)),
            pl.BlockSpec((D_MODEL, tf), lambda i, j: (0, j)),
            pl.BlockSpec((tf, D_MODEL), lambda i, j: (j, 0)),
        ],
        out_specs=pl.BlockSpec((tm, D_MODEL), lambda i, j: (i, 0)),
        compiler_params=pltpu.CompilerParams(
            dimension_semantics=("arbitrary", "arbitrary"),
            vmem_limit_bytes=VMEM_LIMIT),
        name="ffn",
    )(hn, h, wg, wu, wd)


def kernel(x, norm1_g, w_in, q_norm_g, k_norm_g, lambda_q1, lambda_k1, lambda_q2, lambda_k2,
           subln_g, w_branch_a, w_branch_b, w_out, norm2_g, w_ffn_gate, w_ffn_up, w_ffn_down):
    b, s, d = x.shape
    t = b * s
    layer = 0
    x2d = x.reshape(t, d)
    lam_params = jnp.stack([lambda_q1[layer], lambda_k1[layer], lambda_q2[layer], lambda_k2[layer]])

    proj = _in_proj(x2d, norm1_g[layer].reshape(1, d), w_in[layer].astype(BF16),
                    q_norm_g[layer].reshape(1, HEAD_DIM), k_norm_g[layer].reshape(1, HEAD_DIM))
    proj3d = proj.reshape(b, s, IN_WIDTH)

    out_a = _sb_attn(proj3d).reshape(t, SB_WIDTH)
    out_b = _df_attn(proj3d, lam_params, subln_g[layer].reshape(1, DF_V_DIM)).reshape(t, DF_V_WIDTH)

    merged = _merge(out_a, w_branch_a[layer].astype(BF16), out_b, w_branch_b[layer].astype(BF16), proj)
    h, hn = _out_proj(merged, w_out[layer].astype(BF16), x2d, norm2_g[layer].reshape(1, d))
    out = _ffn(hn, h, w_ffn_gate[layer].astype(BF16), w_ffn_up[layer].astype(BF16),
               w_ffn_down[layer].astype(BF16))
    return out.reshape(b, s, d)
```

```python
import math

import jax
import jax.numpy as jnp
from jax import lax
from jax.experimental import pallas as pl
from jax.experimental.pallas import tpu as pltpu

F32 = jnp.float32
BF16 = jnp.bfloat16

D_MODEL = 2048
SB_HEADS = 8
HEAD_DIM = 128
DF_HEADS = 4
DF_V_DIM = 256
CHUNK = 64
SB_WIDTH = SB_HEADS * HEAD_DIM
DF_QK_WIDTH = DF_HEADS * 2 * HEAD_DIM
DF_V_WIDTH = DF_HEADS * DF_V_DIM
IN_WIDTH = 3 * SB_WIDTH + 2 * DF_QK_WIDTH + DF_V_WIDTH + 2 * D_MODEL
EPS = 1e-6
SUBLN_EPS = 1e-5
LAMBDA_INIT = 0.8 - 0.6 * math.exp(-0.3 * 0)
LOG2E = math.log2(math.e)
Q_SCALE = LOG2E / math.sqrt(HEAD_DIM)

OFF_SBQ = 0
OFF_SBK = SB_WIDTH
OFF_SBV = 2 * SB_WIDTH
OFF_DFQ = 3 * SB_WIDTH
OFF_DFK = OFF_DFQ + DF_QK_WIDTH
OFF_DFV = OFF_DFK + DF_QK_WIDTH
OFF_GA = OFF_DFV + DF_V_WIDTH
OFF_GB = OFF_GA + D_MODEL

VMEM_LIMIT = 56 * 1024 * 1024
NEG_BIG = -1e30


def _dot(a, b):
    return jnp.dot(a, b, preferred_element_type=F32)


def _dot_nt(a, b):
    return lax.dot_general(a, b, (((1,), (1,)), ((), ())), preferred_element_type=F32)


PROJ_TN = 1024


def _in_proj_kernel(x_ref, g_ref, w_ref, qg_ref, kg_ref, o_ref, xn_ref):
    j = pl.program_id(1)

    @pl.when(j == 0)
    def _():
        x = x_ref[...]
        ms = jnp.mean(x * x, axis=-1, keepdims=True)
        xn_ref[...] = (x * lax.rsqrt(ms + EPS) * g_ref[...]).astype(BF16)

    is_sbq = j == OFF_SBQ // PROJ_TN
    is_q = j == OFF_DFQ // PROJ_TN
    is_qk = is_q | (j == OFF_DFK // PROJ_TN)
    is_gate = j >= OFF_GA // PROJ_TN

    def proj():
        return _dot(xn_ref[...], w_ref[...])

    @pl.when(jnp.logical_not(is_qk | is_gate))
    def _():
        o_ref[...] = (proj() * jnp.where(is_sbq, Q_SCALE, 1.0)).astype(o_ref.dtype)

    @pl.when(is_qk)
    def _():
        acc = proj()
        gain = jnp.where(is_q, qg_ref[...] * Q_SCALE, kg_ref[...])
        for c in range(PROJ_TN // HEAD_DIM):
            a = acc[:, c * HEAD_DIM:(c + 1) * HEAD_DIM]
            ms = jnp.mean(a * a, axis=-1, keepdims=True)
            o_ref[:, c * HEAD_DIM:(c + 1) * HEAD_DIM] = (
                a * lax.rsqrt(ms + EPS) * gain).astype(o_ref.dtype)

    @pl.when(is_gate)
    def _():
        o_ref[...] = jax.nn.sigmoid(proj()).astype(o_ref.dtype)


def _in_proj(x2d, g1, w_in, qg, kg, tm=1024):
    t = x2d.shape[0]
    grid = (t // tm, IN_WIDTH // PROJ_TN)
    return pl.pallas_call(
        _in_proj_kernel,
        out_shape=jax.ShapeDtypeStruct((t, IN_WIDTH), BF16),
        grid=grid,
        in_specs=[
            pl.BlockSpec((tm, D_MODEL), lambda i, j: (i, 0)),
            pl.BlockSpec((1, D_MODEL), lambda i, j: (0, 0)),
            pl.BlockSpec((D_MODEL, PROJ_TN), lambda i, j: (0, j)),
            pl.BlockSpec((1, HEAD_DIM), lambda i, j: (0, 0)),
            pl.BlockSpec((1, HEAD_DIM), lambda i, j: (0, 0)),
        ],
        out_specs=pl.BlockSpec((tm, PROJ_TN), lambda i, j: (i, j)),
        scratch_shapes=[pltpu.VMEM((tm, D_MODEL), BF16)],
        compiler_params=pltpu.CompilerParams(
            dimension_semantics=("arbitrary", "arbitrary"),
            vmem_limit_bytes=VMEM_LIMIT),
        name="in_proj",
    )(x2d, g1, w_in, qg, kg)


SB_T = 256
SB_G = 4


def _softplus2(a):
    return jnp.maximum(a, 0.0) + jnp.log2(1.0 + jnp.exp2(jnp.minimum(a, -a)))


def _split_hi_lo(a):
    hi = a.astype(BF16)
    lo = (a - hi.astype(F32)).astype(BF16)
    return hi, lo


def _eye(n):
    return (lax.broadcasted_iota(jnp.int32, (n, n), 0)
            == lax.broadcasted_iota(jnp.int32, (n, n), 1)).astype(BF16)


def _fill_vt(v_ref, vt_ref, t):
    eye = _eye(HEAD_DIM)
    n_groups = v_ref.shape[2] // HEAD_DIM

    def body(c, _):
        start = pl.multiple_of(c * t, t)
        for j in range(n_groups):
            lanes = slice(j * HEAD_DIM, (j + 1) * HEAD_DIM)
            vt_ref[c, lanes, :] = _dot_nt(eye, v_ref[0, pl.ds(start, t), lanes]).astype(BF16)
        return 0

    lax.fori_loop(0, v_ref.shape[1] // t, body, 0)


def _sb_attn_kernel(q_ref, k_ref, v_ref, o_ref, vt_ref, a_ref, hl_ref, e_ref):
    qi = pl.program_id(2)
    t = SB_T
    heads = range(SB_G)

    @pl.when(qi == 0)
    def _():
        _fill_vt(v_ref, vt_ref, t)

    r2 = lax.broadcasted_iota(jnp.int32, (t, 2 * t), 0)
    c2 = lax.broadcasted_iota(jnp.int32, (t, 2 * t), 1)
    l2 = ((c2 & (t - 1)) >= r2).astype(BF16)
    lanes = [slice(g * HEAD_DIM, (g + 1) * HEAD_DIM) for g in heads]

    def block_start(j):
        return pl.multiple_of(jnp.maximum(qi - j, 0) * t, t)

    def strict_mask():
        key = lax.broadcasted_iota(jnp.int32, (t, t), 0)
        qry = lax.broadcasted_iota(jnp.int32, (t, t), 1)
        return key < qry

    def s1_matmul(j):
        start = block_start(j)
        return [_dot_nt(k_ref[0, pl.ds(start, t), lanes[g]], q_ref[0, :, lanes[g]]) for g in heads]

    def s1_finish(a, slot, masked):
        for g in heads:
            sp = _softplus2(a[g])
            if masked:
                sp = jnp.where(strict_mask(), sp, 0.0)
            hi, lo = _split_hi_lo(sp)
            a_ref[slot, g] = a[g]
            hl_ref[slot, g, 0:t, :] = hi
            hl_ref[slot, g, t:2 * t, :] = lo

    def s2_matmul(slot):
        return [_dot(l2, hl_ref[slot, g]) for g in heads]

    def s2_finish(cum, slot, masked):
        for g in heads:
            e0 = a_ref[slot, g] - cum[g]
            if masked:
                e0 = jnp.where(strict_mask(), e0, NEG_BIG)
            e_ref[slot, g] = e0
        return tuple(c[0:1, :] for c in cum)

    def s3_weights(slot, carry):
        return [jnp.exp2(e_ref[slot, g] - carry[g]).astype(BF16) for g in heads]

    def s3_matmul(j, w):
        kb = jnp.maximum(qi - j, 0)
        return [_dot(vt_ref[kb, lanes[g], :], w[g]) for g in heads]

    s1_finish(s1_matmul(0), 0, True)
    a = s1_matmul(1)
    cum = s2_matmul(0)
    s1_finish(a, 1, False)
    sums = s2_finish(cum, 0, True)
    init = (tuple(jnp.zeros((HEAD_DIM, t), F32) for _ in heads),
            tuple(jnp.zeros((1, t), F32) for _ in heads), sums)

    def step(i, slot, st):
        acc, carry, sums = st
        w = s3_weights(slot, carry)
        a = s1_matmul(i)
        cum = s2_matmul(1 - slot)
        pv = s3_matmul(i - 2, w)
        s1_finish(a, slot, False)
        new_sums = s2_finish(cum, 1 - slot, False)
        return (tuple(acc[g] + pv[g] for g in heads),
                tuple(carry[g] + sums[g] for g in heads), new_sums)

    def body(k, st):
        return step(2 * k + 3, 1, step(2 * k + 2, 0, st))

    pairs = qi // 2
    acc, carry, sums = lax.fori_loop(0, pairs, body, init)

    w = s3_weights(0, carry)
    cum = s2_matmul(1)
    pv = s3_matmul(2 * pairs, w)
    s2_finish(cum, 1, False)
    acc = tuple(acc[g] + pv[g] for g in heads)
    carry = tuple(carry[g] + sums[g] for g in heads)

    pv = s3_matmul(2 * pairs + 1, s3_weights(1, carry))
    last_is_real = (qi & 1) == 1
    for g in heads:
        out = acc[g] + jnp.where(last_is_real, pv[g], 0.0)
        o_ref[0, :, lanes[g]] = out.T.astype(o_ref.dtype)


def _sb_attn(proj3d):
    b, s, _ = proj3d.shape
    t = SB_T
    w = SB_G * HEAD_DIM
    grid = (b, SB_HEADS // SB_G, s // t)
    cq, ck, cv = OFF_SBQ // w, OFF_SBK // w, OFF_SBV // w
    single = pl.Buffered(1)
    return pl.pallas_call(
        _sb_attn_kernel,
        out_shape=jax.ShapeDtypeStruct((b, s, SB_WIDTH), BF16),
        grid=grid,
        in_specs=[
            pl.BlockSpec((1, t, w), lambda bi, h, qi: (bi, qi, cq + h)),
            pl.BlockSpec((1, s, w), lambda bi, h, qi: (bi, 0, ck + h), pipeline_mode=single),
            pl.BlockSpec((1, s, w), lambda bi, h, qi: (bi, 0, cv + h), pipeline_mode=single),
        ],
        out_specs=pl.BlockSpec((1, t, w), lambda bi, h, qi: (bi, qi, h)),
        scratch_shapes=[pltpu.VMEM((s // t, w, t), BF16),
                        pltpu.VMEM((2, SB_G, t, t), F32),
                        pltpu.VMEM((2, SB_G, 2 * t, t), BF16),
                        pltpu.VMEM((2, SB_G, t, t), F32)],
        compiler_params=pltpu.CompilerParams(
            dimension_semantics=("arbitrary", "arbitrary", "arbitrary"),
            vmem_limit_bytes=VMEM_LIMIT),
        name="sb_attn",
    )(proj3d, proj3d, proj3d)


DF_T = 256
DF_SLOPES = [2.0 ** (-8.0 * (h + 1) / DF_HEADS) for h in range(DF_HEADS)]


def _df_attn_kernel(lamp_ref, q_ref, k_ref, v_ref, sg_ref, o_ref,
                    bias_ref, acc_ref, vt_ref, p_ref, al_ref, m_ref, l_ref):
    bi = pl.program_id(0)
    qi = pl.program_id(1)
    t = DF_T

    @pl.when(qi == 0)
    def _():
        _fill_vt(v_ref, vt_ref, t)

    @pl.when((bi == 0) & (qi == 0))
    def _():
        key = lax.broadcasted_iota(jnp.int32, (t, t), 0)
        qry = lax.broadcasted_iota(jnp.int32, (t, t), 1)
        rel = (qry - key).astype(F32)
        allowed = (key // CHUNK) <= (qry // CHUNK)
        for h in range(DF_HEADS):
            sl = DF_SLOPES[h] * LOG2E
            bias_ref[h, 0] = -sl * rel
            bias_ref[h, 1] = jnp.where(allowed, -sl * jnp.abs(rel), NEG_BIG)

    chains = [(h, c) for h in range(DF_HEADS) for c in range(2)]

    def s1_matmul(j):
        start = pl.multiple_of(j * t, t)
        out = []
        for h, c in chains:
            lanes = slice((2 * h + c) * HEAD_DIM, (2 * h + c + 1) * HEAD_DIM)
            out.append(_dot_nt(k_ref[0, pl.ds(start, t), lanes], q_ref[0, :, lanes]))
        return out

    def s1_finish(a, j, slot):
        which = (j == qi).astype(jnp.int32)
        dist = ((qi - j) * t).astype(F32)
        for i, (h, c) in enumerate(chains):
            row = slice(i, i + 1)
            shift = -(DF_SLOPES[h] * LOG2E) * dist
            s = a[i] + bias_ref[h, which]
            m = m_ref[row, :]
            m_new = jnp.maximum(m, jnp.max(s, axis=0, keepdims=True) + shift)
            al = jnp.exp2(m - m_new)
            p = jnp.exp2(s - (m_new - shift))
            m_ref[row, :] = m_new
            l_ref[row, :] = al * l_ref[row, :] + jnp.sum(p, axis=0, keepdims=True)
            al_ref[slot, row, :] = al
            p_ref[slot, i] = p.astype(BF16)

    def s2_matmul(j, slot):
        return [_dot(vt_ref[j, h * DF_V_DIM:(h + 1) * DF_V_DIM, :], p_ref[slot, i])
                for i, (h, c) in enumerate(chains)]

    def s2_finish(pv, slot):
        for i, (h, c) in enumerate(chains):
            acc_ref[h, c] = al_ref[slot, i:i + 1, :] * acc_ref[h, c] + pv[i]

    def step(j, slot):
        a = s1_matmul(j)
        pv = s2_matmul(j - 1, 1 - slot)
        s2_finish(pv, 1 - slot)
        s1_finish(a, j, slot)

    acc_ref[...] = jnp.zeros_like(acc_ref)
    m_ref[...] = jnp.full_like(m_ref, NEG_BIG)
    l_ref[...] = jnp.zeros_like(l_ref)
    s1_finish(s1_matmul(0), 0, 0)

    def body(k, carry):
        step(2 * k + 1, 1)
        step(2 * k + 2, 0)
        return carry

    lax.fori_loop(0, qi // 2, body, 0)

    @pl.when((qi & 1) == 1)
    def _():
        step(qi, 1)
        s2_finish(s2_matmul(qi, 1), 1)

    @pl.when((qi & 1) == 0)
    def _():
        s2_finish(s2_matmul(qi, 0), 0)

    lp = lamp_ref[...]
    lam = (jnp.exp(jnp.sum(lp[0:1] * lp[1:2], axis=-1, keepdims=True))
           - jnp.exp(jnp.sum(lp[2:3] * lp[3:4], axis=-1, keepdims=True)) + LAMBDA_INIT)
    for h in range(DF_HEADS):
        l1, l2 = l_ref[2 * h:2 * h + 1, :], l_ref[2 * h + 1:2 * h + 2, :]
        out_t = acc_ref[h, 0] / l1 - lam * (acc_ref[h, 1] / l2)
        ms = jnp.mean(out_t * out_t, axis=0, keepdims=True)
        out = (out_t * lax.rsqrt(ms + SUBLN_EPS)).T * (sg_ref[...] * (1.0 - LAMBDA_INIT))
        o_ref[0, :, h * DF_V_DIM:(h + 1) * DF_V_DIM] = out.astype(o_ref.dtype)


def _df_attn(proj3d, lam_params, subln_g):
    b, s, _ = proj3d.shape
    t = DF_T
    grid = (b, s // t)
    cq, ck, cv = OFF_DFQ // DF_QK_WIDTH, OFF_DFK // DF_QK_WIDTH, OFF_DFV // DF_V_WIDTH
    single = pl.Buffered(1)
    return pl.pallas_call(
        _df_attn_kernel,
        out_shape=jax.ShapeDtypeStruct((b, s, DF_V_WIDTH), BF16),
        grid=grid,
        in_specs=[
            pl.BlockSpec((4, HEAD_DIM), lambda bi, qi: (0, 0)),
            pl.BlockSpec((1, t, DF_QK_WIDTH), lambda bi, qi: (bi, qi, cq)),
            pl.BlockSpec((1, s, DF_QK_WIDTH), lambda bi, qi: (bi, 0, ck), pipeline_mode=single),
            pl.BlockSpec((1, s, DF_V_WIDTH), lambda bi, qi: (bi, 0, cv), pipeline_mode=single),
            pl.BlockSpec((1, DF_V_DIM), lambda bi, qi: (0, 0)),
        ],
        out_specs=pl.BlockSpec((1, t, DF_V_WIDTH), lambda bi, qi: (bi, qi, 0)),
        scratch_shapes=[pltpu.VMEM((DF_HEADS, 2, t, t), F32),
                        pltpu.VMEM((DF_HEADS, 2, DF_V_DIM, t), F32),
                        pltpu.VMEM((s // t, DF_V_WIDTH, t), BF16),
                        pltpu.VMEM((2, 2 * DF_HEADS, t, t), BF16),
                        pltpu.VMEM((2, 2 * DF_HEADS, t), F32),
                        pltpu.VMEM((2 * DF_HEADS, t), F32),
                        pltpu.VMEM((2 * DF_HEADS, t), F32)],
        compiler_params=pltpu.CompilerParams(
            dimension_semantics=("arbitrary", "arbitrary"),
            vmem_limit_bytes=VMEM_LIMIT),
        name="df_attn",
    )(lam_params, proj3d, proj3d, proj3d, subln_g)


def _merge_kernel(a_ref, wa_ref, b_ref, wb_ref, ga_ref, gb_ref, o_ref):
    pa = _dot(a_ref[...], wa_ref[...])
    pb = _dot(b_ref[...], wb_ref[...])
    o_ref[...] = (ga_ref[...].astype(F32) * pa + gb_ref[...].astype(F32) * pb).astype(o_ref.dtype)


def _merge(out_a, wa, out_b, wb, proj2d, tm=1024, tn=1024):
    t = out_a.shape[0]
    grid = (t // tm, D_MODEL // tn)
    ca, cb = OFF_GA // tn, OFF_GB // tn
    return pl.pallas_call(
        _merge_kernel,
        out_shape=jax.ShapeDtypeStruct((t, D_MODEL), BF16),
        grid=grid,
        in_specs=[
            pl.BlockSpec((tm, SB_WIDTH), lambda i, j: (i, 0)),
            pl.BlockSpec((SB_WIDTH, tn), lambda i, j: (0, j)),
            pl.BlockSpec((tm, DF_V_WIDTH), lambda i, j: (i, 0)),
            pl.BlockSpec((DF_V_WIDTH, tn), lambda i, j: (0, j)),
            pl.BlockSpec((tm, tn), lambda i, j: (i, ca + j)),
            pl.BlockSpec((tm, tn), lambda i, j: (i, cb + j)),
        ],
        out_specs=pl.BlockSpec((tm, tn), lambda i, j: (i, j)),
        compiler_params=pltpu.CompilerParams(
            dimension_semantics=("arbitrary", "arbitrary"),
            vmem_limit_bytes=VMEM_LIMIT),
        name="merge",
    )(out_a, wa, out_b, wb, proj2d, proj2d)


def _out_proj_kernel(m_ref, w_ref, x_ref, g_ref, h_ref, hn_ref):
    h = x_ref[...] + _dot(m_ref[...], w_ref[...])
    h_ref[...] = h
    ms = jnp.mean(h * h, axis=-1, keepdims=True)
    hn_ref[...] = (h * lax.rsqrt(ms + EPS) * g_ref[...]).astype(hn_ref.dtype)


def _out_proj(merged, w_out, x2d, g2, tm=512):
    t = merged.shape[0]
    grid = (t // tm,)
    return pl.pallas_call(
        _out_proj_kernel,
        out_shape=(jax.ShapeDtypeStruct((t, D_MODEL), F32),
                   jax.ShapeDtypeStruct((t, D_MODEL), BF16)),
        grid=grid,
        in_specs=[
            pl.BlockSpec((tm, D_MODEL), lambda i: (i, 0)),
            pl.BlockSpec((D_MODEL, D_MODEL), lambda i: (0, 0)),
            pl.BlockSpec((tm, D_MODEL), lambda i: (i, 0)),
            pl.BlockSpec((1, D_MODEL), lambda i: (0, 0)),
        ],
        out_specs=(pl.BlockSpec((tm, D_MODEL), lambda i: (i, 0)),
                   pl.BlockSpec((tm, D_MODEL), lambda i: (i, 0))),
        compiler_params=pltpu.CompilerParams(
            dimension_semantics=("arbitrary",),
            vmem_limit_bytes=VMEM_LIMIT),
        name="out_proj",
    )(merged, w_out, x2d, g2)


FFN_SUB = 2


def _ffn_kernel(hn_ref, h_ref, wg_ref, wu_ref, wd_ref, o_ref):
    j = pl.program_id(1)

    @pl.when(j == 0)
    def _():
        o_ref[...] = h_ref[...]

    hn = hn_ref[...]
    tf = wg_ref.shape[1]
    sub = tf // FFN_SUB
    gu = []
    for c in range(FFN_SUB):
        cols = slice(c * sub, (c + 1) * sub)
        gu.append((_dot(hn, wg_ref[:, cols]), _dot(hn, wu_ref[:, cols])))
    acc = None
    for c in range(FFN_SUB):
        g, u = gu[c]
        hid = (g * jax.nn.sigmoid(g) * u).astype(BF16)
        d = _dot(hid, wd_ref[c * sub:(c + 1) * sub, :])
        acc = d if acc is None else acc + d
    o_ref[...] += acc


def _ffn(hn, h, wg, wu, wd, tm=512, tf=512):
    t = hn.shape[0]
    d_ff = wg.shape[1]
    grid = (t // tm, d_ff // tf)
    return pl.pallas_call(
        _ffn_kernel,
        out_shape=jax.ShapeDtypeStruct((t, D_MODEL), F32),
        grid=grid,
        in_specs=[
            pl.BlockSpec((tm, D_MODEL), lambda i, j: (i, 0)),
            pl.BlockSpec((tm, D_MODEL), lambda i, j: (i, 0)),
            pl.BlockSpec((D_MODEL, tf), lambda i, j: (0, j)),
            pl.BlockSpec((D_MODEL, tf), lambda i, j: (0, j)),
            pl.BlockSpec((tf, D_MODEL), lambda i, j: (j, 0)),
        ],
        out_specs=pl.BlockSpec((tm, D_MODEL), lambda i, j: (i, 0)),
        compiler_params=pltpu.CompilerParams(
            dimension_semantics=("arbitrary", "arbitrary"),
            vmem_limit_bytes=VMEM_LIMIT),
        name="ffn",
    )(hn, h, wg, wu, wd)


def kernel(x, norm1_g, w_in, q_norm_g, k_norm_g, lambda_q1, lambda_k1, lambda_q2, lambda_k2,
           subln_g, w_branch_a, w_branch_b, w_out, norm2_g, w_ffn_gate, w_ffn_up, w_ffn_down):
    b, s, d = x.shape
    t = b * s
    layer = 0
    x2d = x.reshape(t, d)
    lam_params = jnp.stack([lambda_q1[layer], lambda_k1[layer], lambda_q2[layer], lambda_k2[layer]])

    proj = _in_proj(x2d, norm1_g[layer].reshape(1, d), w_in[layer].astype(BF16),
                    q_norm_g[layer].reshape(1, HEAD_DIM), k_norm_g[layer].reshape(1, HEAD_DIM))
    proj3d = proj.reshape(b, s, IN_WIDTH)

    out_a = _sb_attn(proj3d).reshape(t, SB_WIDTH)
    out_b = _df_attn(proj3d, lam_params, subln_g[layer].reshape(1, DF_V_DIM)).reshape(t, DF_V_WIDTH)

    merged = _merge(out_a, w_branch_a[layer].astype(BF16), out_b, w_branch_b[layer].astype(BF16), proj)
    h, hn = _out_proj(merged, w_out[layer].astype(BF16), x2d, norm2_g[layer].reshape(1, d))
    out = _ffn(hn, h, w_ffn_gate[layer].astype(BF16), w_ffn_up[layer].astype(BF16),
               w_ffn_down[layer].astype(BF16))
    return out.reshape(b, s, d)
```

```python
import math

import jax
import jax.numpy as jnp
from jax import lax
from jax.experimental import pallas as pl
from jax.experimental.pallas import tpu as pltpu

F32 = jnp.float32
BF16 = jnp.bfloat16

D_MODEL = 2048
SB_HEADS = 8
HEAD_DIM = 128
DF_HEADS = 4
DF_V_DIM = 256
CHUNK = 64
SB_WIDTH = SB_HEADS * HEAD_DIM
DF_QK_WIDTH = DF_HEADS * 2 * HEAD_DIM
DF_V_WIDTH = DF_HEADS * DF_V_DIM
IN_WIDTH = 3 * SB_WIDTH + 2 * DF_QK_WIDTH + DF_V_WIDTH + 2 * D_MODEL
EPS = 1e-6
SUBLN_EPS = 1e-5
LAMBDA_INIT = 0.8 - 0.6 * math.exp(-0.3 * 0)
LOG2E = math.log2(math.e)
Q_SCALE = LOG2E / math.sqrt(HEAD_DIM)

OFF_SBQ = 0
OFF_SBK = SB_WIDTH
OFF_SBV = 2 * SB_WIDTH
OFF_DFQ = 3 * SB_WIDTH
OFF_DFK = OFF_DFQ + DF_QK_WIDTH
OFF_DFV = OFF_DFK + DF_QK_WIDTH
OFF_GA = OFF_DFV + DF_V_WIDTH
OFF_GB = OFF_GA + D_MODEL

VMEM_LIMIT = 56 * 1024 * 1024
NEG_BIG = -1e30


def _dot(a, b):
    return jnp.dot(a, b, preferred_element_type=F32)


def _dot_nt(a, b):
    return lax.dot_general(a, b, (((1,), (1,)), ((), ())), preferred_element_type=F32)


PROJ_TN = 1024


def _in_proj_kernel(x_ref, g_ref, w_ref, qg_ref, kg_ref, o_ref, xn_ref):
    j = pl.program_id(1)

    @pl.when(j == 0)
    def _():
        x = x_ref[...]
        ms = jnp.mean(x * x, axis=-1, keepdims=True)
        xn_ref[...] = (x * lax.rsqrt(ms + EPS) * g_ref[...]).astype(BF16)

    is_sbq = j == OFF_SBQ // PROJ_TN
    is_q = j == OFF_DFQ // PROJ_TN
    is_qk = is_q | (j == OFF_DFK // PROJ_TN)
    is_gate = j >= OFF_GA // PROJ_TN

    def proj():
        return _dot(xn_ref[...], w_ref[...])

    @pl.when(jnp.logical_not(is_qk | is_gate))
    def _():
        o_ref[...] = (proj() * jnp.where(is_sbq, Q_SCALE, 1.0)).astype(o_ref.dtype)

    @pl.when(is_qk)
    def _():
        acc = proj()
        gain = jnp.where(is_q, qg_ref[...] * Q_SCALE, kg_ref[...])
        for c in range(PROJ_TN // HEAD_DIM):
            a = acc[:, c * HEAD_DIM:(c + 1) * HEAD_DIM]
            ms = jnp.mean(a * a, axis=-1, keepdims=True)
            o_ref[:, c * HEAD_DIM:(c + 1) * HEAD_DIM] = (
                a * lax.rsqrt(ms + EPS) * gain).astype(o_ref.dtype)

    @pl.when(is_gate)
    def _():
        o_ref[...] = jax.nn.sigmoid(proj()).astype(o_ref.dtype)


def _in_proj(x2d, g1, w_in, qg, kg, tm=1024):
    t = x2d.shape[0]
    grid = (t // tm, IN_WIDTH // PROJ_TN)
    return pl.pallas_call(
        _in_proj_kernel,
        out_shape=jax.ShapeDtypeStruct((t, IN_WIDTH), BF16),
        grid=grid,
        in_specs=[
            pl.BlockSpec((tm, D_MODEL), lambda i, j: (i, 0)),
            pl.BlockSpec((1, D_MODEL), lambda i, j: (0, 0)),
            pl.BlockSpec((D_MODEL, PROJ_TN), lambda i, j: (0, j)),
            pl.BlockSpec((1, HEAD_DIM), lambda i, j: (0, 0)),
            pl.BlockSpec((1, HEAD_DIM), lambda i, j: (0, 0)),
        ],
        out_specs=pl.BlockSpec((tm, PROJ_TN), lambda i, j: (i, j)),
        scratch_shapes=[pltpu.VMEM((tm, D_MODEL), BF16)],
        compiler_params=pltpu.CompilerParams(
            dimension_semantics=("arbitrary", "arbitrary"),
            vmem_limit_bytes=VMEM_LIMIT),
        name="in_proj",
    )(x2d, g1, w_in, qg, kg)


SB_T = 256
SB_G = 4


def _softplus2(a):
    return jnp.maximum(a, 0.0) + jnp.log2(1.0 + jnp.exp2(jnp.minimum(a, -a)))


def _split_hi_lo(a):
    hi = a.astype(BF16)
    lo = (a - hi.astype(F32)).astype(BF16)
    return hi, lo


def _eye(n):
    return (lax.broadcasted_iota(jnp.int32, (n, n), 0)
            == lax.broadcasted_iota(jnp.int32, (n, n), 1)).astype(BF16)


def _fill_vt(v_ref, vt_ref, t):
    eye = _eye(HEAD_DIM)
    n_groups = v_ref.shape[2] // HEAD_DIM

    def body(c, _):
        start = pl.multiple_of(c * t, t)
        for j in range(n_groups):
            lanes = slice(j * HEAD_DIM, (j + 1) * HEAD_DIM)
            vt_ref[c, lanes, :] = _dot_nt(eye, v_ref[0, pl.ds(start, t), lanes]).astype(BF16)
        return 0

    lax.fori_loop(0, v_ref.shape[1] // t, body, 0)


SB_EXIT_LOG2 = 160.0
NORM_SLACK = 1.01


def _max_key_norm2(k_ref, kn_ref, t):
    ones8 = jnp.ones((8, HEAD_DIM), BF16)
    n_groups = k_ref.shape[2] // HEAD_DIM

    def body(c, mx):
        start = pl.multiple_of(c * t, t)
        out = []
        for g in range(n_groups):
            kf = k_ref[0, pl.ds(start, t), g * HEAD_DIM:(g + 1) * HEAD_DIM].astype(F32)
            out.append(jnp.maximum(mx[g], _dot_nt(ones8, (kf * kf).astype(BF16))))
        return tuple(out)

    mx = lax.fori_loop(0, k_ref.shape[1] // t, body,
                       tuple(jnp.zeros((8, t), F32) for _ in range(n_groups)))
    for g in range(n_groups):
        kn_ref[g:g + 1, :] = jnp.broadcast_to(
            jnp.max(mx[g][0:1, :], axis=1, keepdims=True), (1, HEAD_DIM))


def _sb_attn_kernel(q_ref, k_ref, v_ref, o_ref, vt_ref, a_ref, hl_ref, e_ref, kn_ref):
    qi = pl.program_id(2)
    t = SB_T
    heads = range(SB_G)

    @pl.when(qi == 0)
    def _():
        _fill_vt(v_ref, vt_ref, t)
        _max_key_norm2(k_ref, kn_ref, t)

    r2 = lax.broadcasted_iota(jnp.int32, (t, 2 * t), 0)
    c2 = lax.broadcasted_iota(jnp.int32, (t, 2 * t), 1)
    l2 = ((c2 & (t - 1)) >= r2).astype(BF16)
    lanes = [slice(g * HEAD_DIM, (g + 1) * HEAD_DIM) for g in heads]

    def block_start(j):
        return pl.multiple_of(jnp.maximum(qi - j, 0) * t, t)

    def strict_mask():
        key = lax.broadcasted_iota(jnp.int32, (t, t), 0)
        qry = lax.broadcasted_iota(jnp.int32, (t, t), 1)
        return key < qry

    def s1_matmul(j):
        start = block_start(j)
        return [_dot_nt(k_ref[0, pl.ds(start, t), lanes[g]], q_ref[0, :, lanes[g]]) for g in heads]

    def s1_finish(a, slot, masked):
        for g in heads:
            sp = _softplus2(a[g])
            if masked:
                sp = jnp.where(strict_mask(), sp, 0.0)
            hi, lo = _split_hi_lo(sp)
            a_ref[slot, g] = a[g]
            hl_ref[slot, g, 0:t, :] = hi
            hl_ref[slot, g, t:2 * t, :] = lo

    def s2_matmul(slot):
        return [_dot(l2, hl_ref[slot, g]) for g in heads]

    def s2_finish(cum, slot, masked):
        for g in heads:
            e0 = a_ref[slot, g] - cum[g]
            if masked:
                e0 = jnp.where(strict_mask(), e0, NEG_BIG)
            e_ref[slot, g] = e0
        return tuple(c[0:1, :] for c in cum)

    def s3_weights(slot, carry):
        return [jnp.exp2(e_ref[slot, g] - carry[g]).astype(BF16) for g in heads]

    def s3_matmul(j, w):
        kb = jnp.maximum(qi - j, 0)
        return [_dot(vt_ref[kb, lanes[g], :], w[g]) for g in heads]

    s1_finish(s1_matmul(0), 0, True)
    a = s1_matmul(1)
    cum = s2_matmul(0)
    s1_finish(a, 1, False)
    sums = s2_finish(cum, 0, True)
    init = (tuple(jnp.zeros((HEAD_DIM, t), F32) for _ in heads),
            tuple(jnp.zeros((1, t), F32) for _ in heads), sums)

    def step(i, slot, st):
        acc, carry, sums = st
        w = s3_weights(slot, carry)
        a = s1_matmul(i)
        cum = s2_matmul(1 - slot)
        pv = s3_matmul(i - 2, w)
        s1_finish(a, slot, False)
        new_sums = s2_finish(cum, 1 - slot, False)
        return (tuple(acc[g] + pv[g] for g in heads),
                tuple(carry[g] + sums[g] for g in heads), new_sums)

    ones8 = jnp.ones((8, HEAD_DIM), BF16)
    bound = []
    for g in heads:
        qf = q_ref[0, :, lanes[g]].astype(F32)
        qn2 = _dot_nt(ones8, (qf * qf).astype(BF16))[0:1, :]
        bound.append(jnp.sqrt(qn2 * kn_ref[g:g + 1, 0:1]) * NORM_SLACK)

    def all_underflow(carry):
        slack = bound[0] - carry[0]
        for g in heads[1:]:
            slack = jnp.maximum(slack, bound[g] - carry[g])
        return jnp.max(slack) < -SB_EXIT_LOG2

    pairs = qi // 2

    def cond(st):
        k, done = st[0], st[1]
        return (k < pairs) & jnp.logical_not(done)

    def body(st):
        k = st[0]
        acc, carry, sums = step(2 * k + 3, 1, step(2 * k + 2, 0, st[2:]))
        return (k + 1, all_underflow(carry), acc, carry, sums)

    k, _, acc, carry, sums = lax.while_loop(cond, body, (jnp.int32(0), False) + init)

    w = s3_weights(0, carry)
    cum = s2_matmul(1)
    pv = s3_matmul(2 * k, w)
    s2_finish(cum, 1, False)
    acc = tuple(acc[g] + pv[g] for g in heads)
    carry = tuple(carry[g] + sums[g] for g in heads)

    pv = s3_matmul(2 * k + 1, s3_weights(1, carry))
    last_is_real = 2 * k + 1 <= qi
    for g in heads:
        out = acc[g] + jnp.where(last_is_real, pv[g], 0.0)
        o_ref[0, :, lanes[g]] = out.T.astype(o_ref.dtype)


def _sb_attn(proj3d):
    b, s, _ = proj3d.shape
    t = SB_T
    w = SB_G * HEAD_DIM
    grid = (b, SB_HEADS // SB_G, s // t)
    cq, ck, cv = OFF_SBQ // w, OFF_SBK // w, OFF_SBV // w
    single = pl.Buffered(1)
    return pl.pallas_call(
        _sb_attn_kernel,
        out_shape=jax.ShapeDtypeStruct((b, s, SB_WIDTH), BF16),
        grid=grid,
        in_specs=[
            pl.BlockSpec((1, t, w), lambda bi, h, qi: (bi, qi, cq + h)),
            pl.BlockSpec((1, s, w), lambda bi, h, qi: (bi, 0, ck + h), pipeline_mode=single),
            pl.BlockSpec((1, s, w), lambda bi, h, qi: (bi, 0, cv + h), pipeline_mode=single),
        ],
        out_specs=pl.BlockSpec((1, t, w), lambda bi, h, qi: (bi, qi, h)),
        scratch_shapes=[pltpu.VMEM((s // t, w, t), BF16),
                        pltpu.VMEM((2, SB_G, t, t), F32),
                        pltpu.VMEM((2, SB_G, 2 * t, t), BF16),
                        pltpu.VMEM((2, SB_G, t, t), F32),
                        pltpu.VMEM((SB_G, HEAD_DIM), F32)],
        compiler_params=pltpu.CompilerParams(
            dimension_semantics=("arbitrary", "arbitrary", "arbitrary"),
            vmem_limit_bytes=VMEM_LIMIT),
        name="sb_attn",
    )(proj3d, proj3d, proj3d)


DF_T = 256
DF_SLOPES = [2.0 ** (-8.0 * (h + 1) / DF_HEADS) for h in range(DF_HEADS)]


def _df_attn_kernel(lamp_ref, q_ref, k_ref, v_ref, sg_ref, o_ref,
                    bias_ref, acc_ref, vt_ref, p_ref, al_ref, m_ref, l_ref):
    bi = pl.program_id(0)
    qi = pl.program_id(1)
    t = DF_T

    @pl.when(qi == 0)
    def _():
        _fill_vt(v_ref, vt_ref, t)

    @pl.when((bi == 0) & (qi == 0))
    def _():
        key = lax.broadcasted_iota(jnp.int32, (t, t), 0)
        qry = lax.broadcasted_iota(jnp.int32, (t, t), 1)
        rel = (qry - key).astype(F32)
        allowed = (key // CHUNK) <= (qry // CHUNK)
        for h in range(DF_HEADS):
            sl = DF_SLOPES[h] * LOG2E
            bias_ref[h, 0] = -sl * rel
            bias_ref[h, 1] = jnp.where(allowed, -sl * jnp.abs(rel), NEG_BIG)

    chains = [(h, c) for h in range(DF_HEADS) for c in range(2)]

    def s1_matmul(j):
        start = pl.multiple_of(j * t, t)
        out = []
        for h, c in chains:
            lanes = slice((2 * h + c) * HEAD_DIM, (2 * h + c + 1) * HEAD_DIM)
            out.append(_dot_nt(k_ref[0, pl.ds(start, t), lanes], q_ref[0, :, lanes]))
        return out

    def s1_finish(a, j, slot):
        which = (j == qi).astype(jnp.int32)
        dist = ((qi - j) * t).astype(F32)
        for i, (h, c) in enumerate(chains):
            row = slice(i, i + 1)
            shift = -(DF_SLOPES[h] * LOG2E) * dist
            s = a[i] + bias_ref[h, which]
            m = m_ref[row, :]
            m_new = jnp.maximum(m, jnp.max(s, axis=0, keepdims=True) + shift)
            al = jnp.exp2(m - m_new)
            p = jnp.exp2(s - (m_new - shift))
            m_ref[row, :] = m_new
            l_ref[row, :] = al * l_ref[row, :] + jnp.sum(p, axis=0, keepdims=True)
            al_ref[slot, row, :] = al
            p_ref[slot, i] = p.astype(BF16)

    def s2_matmul(j, slot):
        return [_dot(vt_ref[j, h * DF_V_DIM:(h + 1) * DF_V_DIM, :], p_ref[slot, i])
                for i, (h, c) in enumerate(chains)]

    def s2_finish(pv, slot):
        for i, (h, c) in enumerate(chains):
            acc_ref[h, c] = al_ref[slot, i:i + 1, :] * acc_ref[h, c] + pv[i]

    def step(j, slot):
        a = s1_matmul(j)
        pv = s2_matmul(j - 1, 1 - slot)
        s2_finish(pv, 1 - slot)
        s1_finish(a, j, slot)

    acc_ref[...] = jnp.zeros_like(acc_ref)
    m_ref[...] = jnp.full_like(m_ref, NEG_BIG)
    l_ref[...] = jnp.zeros_like(l_ref)
    s1_finish(s1_matmul(0), 0, 0)

    def body(k, carry):
        step(2 * k + 1, 1)
        step(2 * k + 2, 0)
        return carry

    lax.fori_loop(0, qi // 2, body, 0)

    @pl.when((qi & 1) == 1)
    def _():
        step(qi, 1)
        s2_finish(s2_matmul(qi, 1), 1)

    @pl.when((qi & 1) == 0)
    def _():
        s2_finish(s2_matmul(qi, 0), 0)

    lp = lamp_ref[...]
    lam = (jnp.exp(jnp.sum(lp[0:1] * lp[1:2], axis=-1, keepdims=True))
           - jnp.exp(jnp.sum(lp[2:3] * lp[3:4], axis=-1, keepdims=True)) + LAMBDA_INIT)
    for h in range(DF_HEADS):
        l1, l2 = l_ref[2 * h:2 * h + 1, :], l_ref[2 * h + 1:2 * h + 2, :]
        out_t = acc_ref[h, 0] / l1 - lam * (acc_ref[h, 1] / l2)
        ms = jnp.mean(out_t * out_t, axis=0, keepdims=True)
        out = (out_t * lax.rsqrt(ms + SUBLN_EPS)).T * (sg_ref[...] * (1.0 - LAMBDA_INIT))
        o_ref[0, :, h * DF_V_DIM:(h + 1) * DF_V_DIM] = out.astype(o_ref.dtype)


def _df_attn(proj3d, lam_params, subln_g):
    b, s, _ = proj3d.shape
    t = DF_T
    grid = (b, s // t)
    cq, ck, cv = OFF_DFQ // DF_QK_WIDTH, OFF_DFK // DF_QK_WIDTH, OFF_DFV // DF_V_WIDTH
    single = pl.Buffered(1)
    return pl.pallas_call(
        _df_attn_kernel,
        out_shape=jax.ShapeDtypeStruct((b, s, DF_V_WIDTH), BF16),
        grid=grid,
        in_specs=[
            pl.BlockSpec((4, HEAD_DIM), lambda bi, qi: (0, 0)),
            pl.BlockSpec((1, t, DF_QK_WIDTH), lambda bi, qi: (bi, qi, cq)),
            pl.BlockSpec((1, s, DF_QK_WIDTH), lambda bi, qi: (bi, 0, ck), pipeline_mode=single),
            pl.BlockSpec((1, s, DF_V_WIDTH), lambda bi, qi: (bi, 0, cv), pipeline_mode=single),
            pl.BlockSpec((1, DF_V_DIM), lambda bi, qi: (0, 0)),
        ],
        out_specs=pl.BlockSpec((1, t, DF_V_WIDTH), lambda bi, qi: (bi, qi, 0)),
        scratch_shapes=[pltpu.VMEM((DF_HEADS, 2, t, t), F32),
                        pltpu.VMEM((DF_HEADS, 2, DF_V_DIM, t), F32),
                        pltpu.VMEM((s // t, DF_V_WIDTH, t), BF16),
                        pltpu.VMEM((2, 2 * DF_HEADS, t, t), BF16),
                        pltpu.VMEM((2, 2 * DF_HEADS, t), F32),
                        pltpu.VMEM((2 * DF_HEADS, t), F32),
                        pltpu.VMEM((2 * DF_HEADS, t), F32)],
        compiler_params=pltpu.CompilerParams(
            dimension_semantics=("arbitrary", "arbitrary"),
            vmem_limit_bytes=VMEM_LIMIT),
        name="df_attn",
    )(lam_params, proj3d, proj3d, proj3d, subln_g)


def _merge_kernel(a_ref, wa_ref, b_ref, wb_ref, ga_ref, gb_ref, o_ref):
    pa = _dot(a_ref[...], wa_ref[...])
    pb = _dot(b_ref[...], wb_ref[...])
    o_ref[...] = (ga_ref[...].astype(F32) * pa + gb_ref[...].astype(F32) * pb).astype(o_ref.dtype)


def _merge(out_a, wa, out_b, wb, proj2d, tm=1024, tn=1024):
    t = out_a.shape[0]
    grid = (t // tm, D_MODEL // tn)
    ca, cb = OFF_GA // tn, OFF_GB // tn
    return pl.pallas_call(
        _merge_kernel,
        out_shape=jax.ShapeDtypeStruct((t, D_MODEL), BF16),
        grid=grid,
        in_specs=[
            pl.BlockSpec((tm, SB_WIDTH), lambda i, j: (i, 0)),
            pl.BlockSpec((SB_WIDTH, tn), lambda i, j: (0, j)),
            pl.BlockSpec((tm, DF_V_WIDTH), lambda i, j: (i, 0)),
            pl.BlockSpec((DF_V_WIDTH, tn), lambda i, j: (0, j)),
            pl.BlockSpec((tm, tn), lambda i, j: (i, ca + j)),
            pl.BlockSpec((tm, tn), lambda i, j: (i, cb + j)),
        ],
        out_specs=pl.BlockSpec((tm, tn), lambda i, j: (i, j)),
        compiler_params=pltpu.CompilerParams(
            dimension_semantics=("arbitrary", "arbitrary"),
            vmem_limit_bytes=VMEM_LIMIT),
        name="merge",
    )(out_a, wa, out_b, wb, proj2d, proj2d)


def _out_proj_kernel(m_ref, w_ref, x_ref, g_ref, h_ref, hn_ref):
    h = x_ref[...] + _dot(m_ref[...], w_ref[...])
    h_ref[...] = h
    ms = jnp.mean(h * h, axis=-1, keepdims=True)
    hn_ref[...] = (h * lax.rsqrt(ms + EPS) * g_ref[...]).astype(hn_ref.dtype)


def _out_proj(merged, w_out, x2d, g2, tm=512):
    t = merged.shape[0]
    grid = (t // tm,)
    return pl.pallas_call(
        _out_proj_kernel,
        out_shape=(jax.ShapeDtypeStruct((t, D_MODEL), F32),
                   jax.ShapeDtypeStruct((t, D_MODEL), BF16)),
        grid=grid,
        in_specs=[
            pl.BlockSpec((tm, D_MODEL), lambda i: (i, 0)),
            pl.BlockSpec((D_MODEL, D_MODEL), lambda i: (0, 0)),
            pl.BlockSpec((tm, D_MODEL), lambda i: (i, 0)),
            pl.BlockSpec((1, D_MODEL), lambda i: (0, 0)),
        ],
        out_specs=(pl.BlockSpec((tm, D_MODEL), lambda i: (i, 0)),
                   pl.BlockSpec((tm, D_MODEL), lambda i: (i, 0))),
        compiler_params=pltpu.CompilerParams(
            dimension_semantics=("arbitrary",),
            vmem_limit_bytes=VMEM_LIMIT),
        name="out_proj",
    )(merged, w_out, x2d, g2)


FFN_SUB = 2


def _ffn_kernel(hn_ref, h_ref, wg_ref, wu_ref, wd_ref, o_ref):
    j = pl.program_id(1)

    @pl.when(j == 0)
    def _():
        o_ref[...] = h_ref[...]

    hn = hn_ref[...]
    tf = wg_ref.shape[1]
    sub = tf // FFN_SUB
    gu = []
    for c in range(FFN_SUB):
        cols = slice(c * sub, (c + 1) * sub)
        gu.append((_dot(hn, wg_ref[:, cols]), _dot(hn, wu_ref[:, cols])))
    acc = None
    for c in range(FFN_SUB):
        g, u = gu[c]
        hid = (g * jax.nn.sigmoid(g) * u).astype(BF16)
        d = _dot(hid, wd_ref[c * sub:(c + 1) * sub, :])
        acc = d if acc is None else acc + d
    o_ref[...] += acc


def _ffn(hn, h, wg, wu, wd, tm=512, tf=512):
    t = hn.shape[0]
    d_ff = wg.shape[1]
    grid = (t // tm, d_ff // tf)
    return pl.pallas_call(
        _ffn_kernel,
        out_shape=jax.ShapeDtypeStruct((t, D_MODEL), F32),
        grid=grid,
        in_specs=[
            pl.BlockSpec((tm, D_MODEL), lambda i, j: (i, 0)),
            pl.BlockSpec((tm, D_MODEL), lambda i, j: (i, 0)),
            pl.BlockSpec((D_MODEL, tf), lambda i, j: (0, j)),
            pl.BlockSpec((D_MODEL, tf), lambda i, j: (0, j)),
            pl.BlockSpec((tf, D_MODEL), lambda i, j: (j, 0)),
        ],
        out_specs=pl.BlockSpec((tm, D_MODEL), lambda i, j: (i, 0)),
        compiler_params=pltpu.CompilerParams(
            dimension_semantics=("arbitrary", "arbitrary"),
            vmem_limit_bytes=VMEM_LIMIT),
        name="ffn",
    )(hn, h, wg, wu, wd)


def kernel(x, norm1_g, w_in, q_norm_g, k_norm_g, lambda_q1, lambda_k1, lambda_q2, lambda_k2,
           subln_g, w_branch_a, w_branch_b, w_out, norm2_g, w_ffn_gate, w_ffn_up, w_ffn_down):
    b, s, d = x.shape
    t = b * s
    layer = 0
    x2d = x.reshape(t, d)
    lam_params = jnp.stack([lambda_q1[layer], lambda_k1[layer], lambda_q2[layer], lambda_k2[layer]])

    proj = _in_proj(x2d, norm1_g[layer].reshape(1, d), w_in[layer].astype(BF16),
                    q_norm_g[layer].reshape(1, HEAD_DIM), k_norm_g[layer].reshape(1, HEAD_DIM))
    proj3d = proj.reshape(b, s, IN_WIDTH)

    out_a = _sb_attn(proj3d).reshape(t, SB_WIDTH)
    out_b = _df_attn(proj3d, lam_params, subln_g[layer].reshape(1, DF_V_DIM)).reshape(t, DF_V_WIDTH)

    merged = _merge(out_a, w_branch_a[layer].astype(BF16), out_b, w_branch_b[layer].astype(BF16), proj)
    h, hn = _out_proj(merged, w_out[layer].astype(BF16), x2d, norm2_g[layer].reshape(1, d))
    out = _ffn(hn, h, w_ffn_gate[layer].astype(BF16), w_ffn_up[layer].astype(BF16),
               w_ffn_down[layer].astype(BF16))
    return out.reshape(b, s, d)
```

```python
import math

import jax
import jax.numpy as jnp
from jax import lax
from jax.experimental import pallas as pl
from jax.experimental.pallas import tpu as pltpu

F32 = jnp.float32
BF16 = jnp.bfloat16

D_MODEL = 2048
SB_HEADS = 8
HEAD_DIM = 128
DF_HEADS = 4
DF_V_DIM = 256
CHUNK = 64
SB_WIDTH = SB_HEADS * HEAD_DIM
DF_QK_WIDTH = DF_HEADS * 2 * HEAD_DIM
DF_V_WIDTH = DF_HEADS * DF_V_DIM
IN_WIDTH = 3 * SB_WIDTH + 2 * DF_QK_WIDTH + DF_V_WIDTH + 2 * D_MODEL
EPS = 1e-6
SUBLN_EPS = 1e-5
LAMBDA_INIT = 0.8 - 0.6 * math.exp(-0.3 * 0)
LOG2E = math.log2(math.e)
Q_SCALE = LOG2E / math.sqrt(HEAD_DIM)

OFF_SBQ = 0
OFF_SBK = SB_WIDTH
OFF_SBV = 2 * SB_WIDTH
OFF_DFQ = 3 * SB_WIDTH
OFF_DFK = OFF_DFQ + DF_QK_WIDTH
OFF_DFV = OFF_DFK + DF_QK_WIDTH
OFF_GA = OFF_DFV + DF_V_WIDTH
OFF_GB = OFF_GA + D_MODEL

VMEM_LIMIT = 56 * 1024 * 1024
NEG_BIG = -1e30


def _dot(a, b):
    return jnp.dot(a, b, preferred_element_type=F32)


def _dot_nt(a, b):
    return lax.dot_general(a, b, (((1,), (1,)), ((), ())), preferred_element_type=F32)


PROJ_TN = 1024


def _in_proj_kernel(x_ref, g_ref, w_ref, qg_ref, kg_ref, o_ref, xn_ref):
    j = pl.program_id(1)

    @pl.when(j == 0)
    def _():
        x = x_ref[...]
        ms = jnp.mean(x * x, axis=-1, keepdims=True)
        xn_ref[...] = (x * lax.rsqrt(ms + EPS) * g_ref[...]).astype(BF16)

    is_sbq = j == OFF_SBQ // PROJ_TN
    is_q = j == OFF_DFQ // PROJ_TN
    is_qk = is_q | (j == OFF_DFK // PROJ_TN)
    is_gate = j >= OFF_GA // PROJ_TN

    def proj():
        return _dot(xn_ref[...], w_ref[...])

    @pl.when(jnp.logical_not(is_qk | is_gate))
    def _():
        o_ref[...] = (proj() * jnp.where(is_sbq, Q_SCALE, 1.0)).astype(o_ref.dtype)

    @pl.when(is_qk)
    def _():
        acc = proj()
        gain = jnp.where(is_q, qg_ref[...] * Q_SCALE, kg_ref[...])
        for c in range(PROJ_TN // HEAD_DIM):
            a = acc[:, c * HEAD_DIM:(c + 1) * HEAD_DIM]
            ms = jnp.mean(a * a, axis=-1, keepdims=True)
            o_ref[:, c * HEAD_DIM:(c + 1) * HEAD_DIM] = (
                a * lax.rsqrt(ms + EPS) * gain).astype(o_ref.dtype)

    @pl.when(is_gate)
    def _():
        o_ref[...] = jax.nn.sigmoid(proj()).astype(o_ref.dtype)


def _in_proj(x2d, g1, w_in, qg, kg, tm=1024):
    t = x2d.shape[0]
    grid = (t // tm, IN_WIDTH // PROJ_TN)
    return pl.pallas_call(
        _in_proj_kernel,
        out_shape=jax.ShapeDtypeStruct((t, IN_WIDTH), BF16),
        grid=grid,
        in_specs=[
            pl.BlockSpec((tm, D_MODEL), lambda i, j: (i, 0)),
            pl.BlockSpec((1, D_MODEL), lambda i, j: (0, 0)),
            pl.BlockSpec((D_MODEL, PROJ_TN), lambda i, j: (0, j)),
            pl.BlockSpec((1, HEAD_DIM), lambda i, j: (0, 0)),
            pl.BlockSpec((1, HEAD_DIM), lambda i, j: (0, 0)),
        ],
        out_specs=pl.BlockSpec((tm, PROJ_TN), lambda i, j: (i, j)),
        scratch_shapes=[pltpu.VMEM((tm, D_MODEL), BF16)],
        compiler_params=pltpu.CompilerParams(
            dimension_semantics=("arbitrary", "arbitrary"),
            vmem_limit_bytes=VMEM_LIMIT),
        name="in_proj",
    )(x2d, g1, w_in, qg, kg)


SB_T = 256
SB_G = 4


def _softplus2(a):
    return jnp.maximum(a, 0.0) + jnp.log2(1.0 + jnp.exp2(jnp.minimum(a, -a)))


def _split_hi_lo(a):
    hi = a.astype(BF16)
    lo = (a - hi.astype(F32)).astype(BF16)
    return hi, lo


def _eye(n):
    return (lax.broadcasted_iota(jnp.int32, (n, n), 0)
            == lax.broadcasted_iota(jnp.int32, (n, n), 1)).astype(BF16)


def _fill_vt(v_ref, vt_ref, t):
    eye = _eye(HEAD_DIM)
    n_groups = v_ref.shape[2] // HEAD_DIM

    def body(c, _):
        start = pl.multiple_of(c * t, t)
        for j in range(n_groups):
            lanes = slice(j * HEAD_DIM, (j + 1) * HEAD_DIM)
            vt_ref[c, lanes, :] = _dot_nt(eye, v_ref[0, pl.ds(start, t), lanes]).astype(BF16)
        return 0

    lax.fori_loop(0, v_ref.shape[1] // t, body, 0)


SB_EXIT_LOG2 = 160.0
NORM_SLACK = 1.01


def _max_key_norm2(k_ref, kn_ref, t):
    ones8 = jnp.ones((8, HEAD_DIM), BF16)
    n_groups = k_ref.shape[2] // HEAD_DIM

    def body(c, mx):
        start = pl.multiple_of(c * t, t)
        out = []
        for g in range(n_groups):
            kf = k_ref[0, pl.ds(start, t), g * HEAD_DIM:(g + 1) * HEAD_DIM].astype(F32)
            out.append(jnp.maximum(mx[g], _dot_nt(ones8, (kf * kf).astype(BF16))))
        return tuple(out)

    mx = lax.fori_loop(0, k_ref.shape[1] // t, body,
                       tuple(jnp.zeros((8, t), F32) for _ in range(n_groups)))
    for g in range(n_groups):
        kn_ref[g:g + 1, :] = jnp.broadcast_to(
            jnp.max(mx[g][0:1, :], axis=1, keepdims=True), (1, HEAD_DIM))


def _sb_attn_kernel(q_ref, k_ref, v_ref, o_ref, vt_ref, a_ref, hl_ref, e_ref, kn_ref):
    qi = pl.program_id(2)
    t = SB_T
    heads = range(SB_G)

    @pl.when(qi == 0)
    def _():
        _fill_vt(v_ref, vt_ref, t)
        _max_key_norm2(k_ref, kn_ref, t)

    r2 = lax.broadcasted_iota(jnp.int32, (t, 2 * t), 0)
    c2 = lax.broadcasted_iota(jnp.int32, (t, 2 * t), 1)
    l2 = ((c2 & (t - 1)) >= r2).astype(BF16)
    lanes = [slice(g * HEAD_DIM, (g + 1) * HEAD_DIM) for g in heads]

    def block_start(j):
        return pl.multiple_of(jnp.maximum(qi - j, 0) * t, t)

    def strict_mask():
        key = lax.broadcasted_iota(jnp.int32, (t, t), 0)
        qry = lax.broadcasted_iota(jnp.int32, (t, t), 1)
        return key < qry

    def s1_matmul(j):
        start = block_start(j)
        return [_dot_nt(k_ref[0, pl.ds(start, t), lanes[g]], q_ref[0, :, lanes[g]]) for g in heads]

    def s1_finish(a, slot, masked):
        for g in heads:
            sp = _softplus2(a[g])
            if masked:
                sp = jnp.where(strict_mask(), sp, 0.0)
            hi, lo = _split_hi_lo(sp)
            a_ref[slot, g] = a[g]
            hl_ref[slot, g, 0:t, :] = hi
            hl_ref[slot, g, t:2 * t, :] = lo

    def s2_matmul(slot):
        return [_dot(l2, hl_ref[slot, g]) for g in heads]

    def s2_finish(cum, slot, masked):
        for g in heads:
            e0 = a_ref[slot, g] - cum[g]
            if masked:
                e0 = jnp.where(strict_mask(), e0, NEG_BIG)
            e_ref[slot, g] = e0
        return tuple(c[0:1, :] for c in cum)

    def s3_weights(slot, carry):
        return [jnp.exp2(e_ref[slot, g] - carry[g]).astype(BF16) for g in heads]

    def s3_matmul(j, w):
        kb = jnp.maximum(qi - j, 0)
        return [_dot(vt_ref[kb, lanes[g], :], w[g]) for g in heads]

    def step(i, slot, st):
        acc, carry, sums = st
        w = s3_weights(slot, carry)
        a = s1_matmul(i)
        cum = s2_matmul(1 - slot)
        pv = s3_matmul(i - 2, w)
        s1_finish(a, slot, False)
        new_sums = s2_finish(cum, 1 - slot, False)
        return (tuple(acc[g] + pv[g] for g in heads),
                tuple(carry[g] + sums[g] for g in heads), new_sums)

    ones8 = jnp.ones((8, HEAD_DIM), BF16)
    bound = []
    for g in heads:
        qf = q_ref[0, :, lanes[g]].astype(F32)
        qn2 = _dot_nt(ones8, (qf * qf).astype(BF16))[0:1, :]
        bound.append(jnp.sqrt(qn2 * kn_ref[g:g + 1, 0:1]) * NORM_SLACK)

    def all_underflow(carry):
        slack = bound[0] - carry[0]
        for g in heads[1:]:
            slack = jnp.maximum(slack, bound[g] - carry[g])
        return jnp.max(slack) < -SB_EXIT_LOG2

    a0 = s1_matmul(0)
    a1 = s1_matmul(1)
    s1_finish(a0, 0, True)
    s1_finish(a1, 1, False)
    cum0 = s2_matmul(0)
    cum1 = s2_matmul(1)
    sums0 = s2_finish(cum0, 0, True)
    sums1 = s2_finish(cum1, 1, False)
    w0 = s3_weights(0, tuple(jnp.zeros((1, t), F32) for _ in heads))
    w1 = s3_weights(1, sums0)
    pv0 = s3_matmul(0, w0)
    pv1 = s3_matmul(1, w1)
    acc2 = tuple(pv0[g] + jnp.where(qi > 0, pv1[g], 0.0) for g in heads)
    carry2 = tuple(sums0[g] + sums1[g] for g in heads)

    def remaining_blocks(acc, carry):
        s1_finish(s1_matmul(2), 0, False)
        a = s1_matmul(3)
        cum = s2_matmul(0)
        s1_finish(a, 1, False)
        sums = s2_finish(cum, 0, False)
        pairs = (qi - 2) // 2

        def cond(st):
            k, done = st[0], st[1]
            return (k < pairs) & jnp.logical_not(done)

        def body(st):
            k = st[0]
            acc, carry, sums = step(2 * k + 5, 1, step(2 * k + 4, 0, st[2:]))
            return (k + 1, all_underflow(carry), acc, carry, sums)

        k, _, acc, carry, sums = lax.while_loop(
            cond, body, (jnp.int32(0), False, acc, carry, sums))

        w = s3_weights(0, carry)
        cum = s2_matmul(1)
        pv = s3_matmul(2 * k + 2, w)
        s2_finish(cum, 1, False)
        acc = tuple(acc[g] + pv[g] for g in heads)
        carry = tuple(carry[g] + sums[g] for g in heads)
        pv = s3_matmul(2 * k + 3, s3_weights(1, carry))
        last_is_real = 2 * k + 3 <= qi
        return tuple(acc[g] + jnp.where(last_is_real, pv[g], 0.0) for g in heads)

    more = (qi >= 2) & jnp.logical_not(all_underflow(carry2))
    acc = lax.cond(more, lambda: remaining_blocks(acc2, carry2), lambda: acc2)
    for g in heads:
        o_ref[0, :, lanes[g]] = acc[g].T.astype(o_ref.dtype)


def _sb_attn(proj3d):
    b, s, _ = proj3d.shape
    t = SB_T
    w = SB_G * HEAD_DIM
    grid = (b, SB_HEADS // SB_G, s // t)
    cq, ck, cv = OFF_SBQ // w, OFF_SBK // w, OFF_SBV // w
    single = pl.Buffered(1)
    return pl.pallas_call(
        _sb_attn_kernel,
        out_shape=jax.ShapeDtypeStruct((b, s, SB_WIDTH), BF16),
        grid=grid,
        in_specs=[
            pl.BlockSpec((1, t, w), lambda bi, h, qi: (bi, qi, cq + h)),
            pl.BlockSpec((1, s, w), lambda bi, h, qi: (bi, 0, ck + h), pipeline_mode=single),
            pl.BlockSpec((1, s, w), lambda bi, h, qi: (bi, 0, cv + h), pipeline_mode=single),
        ],
        out_specs=pl.BlockSpec((1, t, w), lambda bi, h, qi: (bi, qi, h)),
        scratch_shapes=[pltpu.VMEM((s // t, w, t), BF16),
                        pltpu.VMEM((2, SB_G, t, t), F32),
                        pltpu.VMEM((2, SB_G, 2 * t, t), BF16),
                        pltpu.VMEM((2, SB_G, t, t), F32),
                        pltpu.VMEM((SB_G, HEAD_DIM), F32)],
        compiler_params=pltpu.CompilerParams(
            dimension_semantics=("arbitrary", "arbitrary", "arbitrary"),
            vmem_limit_bytes=VMEM_LIMIT),
        name="sb_attn",
    )(proj3d, proj3d, proj3d)


DF_T = 256
DF_SLOPES = [2.0 ** (-8.0 * (h + 1) / DF_HEADS) for h in range(DF_HEADS)]


def _df_attn_kernel(lamp_ref, q_ref, k_ref, v_ref, sg_ref, o_ref,
                    bias_ref, acc_ref, vt_ref, p_ref, al_ref, m_ref, l_ref):
    bi = pl.program_id(0)
    qi = pl.program_id(1)
    t = DF_T

    @pl.when(qi == 0)
    def _():
        _fill_vt(v_ref, vt_ref, t)

    @pl.when((bi == 0) & (qi == 0))
    def _():
        key = lax.broadcasted_iota(jnp.int32, (t, t), 0)
        qry = lax.broadcasted_iota(jnp.int32, (t, t), 1)
        rel = (qry - key).astype(F32)
        allowed = (key // CHUNK) <= (qry // CHUNK)
        for h in range(DF_HEADS):
            sl = DF_SLOPES[h] * LOG2E
            bias_ref[h, 0] = -sl * rel
            bias_ref[h, 1] = jnp.where(allowed, -sl * jnp.abs(rel), NEG_BIG)

    chains = [(h, c) for h in range(DF_HEADS) for c in range(2)]

    def s1_matmul(j):
        start = pl.multiple_of(j * t, t)
        out = []
        for h, c in chains:
            lanes = slice((2 * h + c) * HEAD_DIM, (2 * h + c + 1) * HEAD_DIM)
            out.append(_dot_nt(k_ref[0, pl.ds(start, t), lanes], q_ref[0, :, lanes]))
        return out

    def s1_finish(a, j, slot):
        which = (j == qi).astype(jnp.int32)
        dist = ((qi - j) * t).astype(F32)
        for i, (h, c) in enumerate(chains):
            row = slice(i, i + 1)
            shift = -(DF_SLOPES[h] * LOG2E) * dist
            s = a[i] + bias_ref[h, which]
            m = m_ref[row, :]
            m_new = jnp.maximum(m, jnp.max(s, axis=0, keepdims=True) + shift)
            al = jnp.exp2(m - m_new)
            p = jnp.exp2(s - (m_new - shift))
            m_ref[row, :] = m_new
            l_ref[row, :] = al * l_ref[row, :] + jnp.sum(p, axis=0, keepdims=True)
            al_ref[slot, row, :] = al
            p_ref[slot, i] = p.astype(BF16)

    def s2_matmul(j, slot):
        return [_dot(vt_ref[j, h * DF_V_DIM:(h + 1) * DF_V_DIM, :], p_ref[slot, i])
                for i, (h, c) in enumerate(chains)]

    def s2_finish(pv, slot):
        for i, (h, c) in enumerate(chains):
            acc_ref[h, c] = al_ref[slot, i:i + 1, :] * acc_ref[h, c] + pv[i]

    def step(j, slot):
        a = s1_matmul(j)
        pv = s2_matmul(j - 1, 1 - slot)
        s2_finish(pv, 1 - slot)
        s1_finish(a, j, slot)

    acc_ref[...] = jnp.zeros_like(acc_ref)
    m_ref[...] = jnp.full_like(m_ref, NEG_BIG)
    l_ref[...] = jnp.zeros_like(l_ref)
    s1_finish(s1_matmul(0), 0, 0)

    def body(k, carry):
        step(2 * k + 1, 1)
        step(2 * k + 2, 0)
        return carry

    lax.fori_loop(0, qi // 2, body, 0)

    @pl.when((qi & 1) == 1)
    def _():
        step(qi, 1)
        s2_finish(s2_matmul(qi, 1), 1)

    @pl.when((qi & 1) == 0)
    def _():
        s2_finish(s2_matmul(qi, 0), 0)

    lp = lamp_ref[...]
    lam = (jnp.exp(jnp.sum(lp[0:1] * lp[1:2], axis=-1, keepdims=True))
           - jnp.exp(jnp.sum(lp[2:3] * lp[3:4], axis=-1, keepdims=True)) + LAMBDA_INIT)
    for h in range(DF_HEADS):
        l1, l2 = l_ref[2 * h:2 * h + 1, :], l_ref[2 * h + 1:2 * h + 2, :]
        out_t = acc_ref[h, 0] / l1 - lam * (acc_ref[h, 1] / l2)
        ms = jnp.mean(out_t * out_t, axis=0, keepdims=True)
        out = (out_t * lax.rsqrt(ms + SUBLN_EPS)).T * (sg_ref[...] * (1.0 - LAMBDA_INIT))
        o_ref[0, :, h * DF_V_DIM:(h + 1) * DF_V_DIM] = out.astype(o_ref.dtype)


def _df_attn(proj3d, lam_params, subln_g):
    b, s, _ = proj3d.shape
    t = DF_T
    grid = (b, s // t)
    cq, ck, cv = OFF_DFQ // DF_QK_WIDTH, OFF_DFK // DF_QK_WIDTH, OFF_DFV // DF_V_WIDTH
    single = pl.Buffered(1)
    return pl.pallas_call(
        _df_attn_kernel,
        out_shape=jax.ShapeDtypeStruct((b, s, DF_V_WIDTH), BF16),
        grid=grid,
        in_specs=[
            pl.BlockSpec((4, HEAD_DIM), lambda bi, qi: (0, 0)),
            pl.BlockSpec((1, t, DF_QK_WIDTH), lambda bi, qi: (bi, qi, cq)),
            pl.BlockSpec((1, s, DF_QK_WIDTH), lambda bi, qi: (bi, 0, ck), pipeline_mode=single),
            pl.BlockSpec((1, s, DF_V_WIDTH), lambda bi, qi: (bi, 0, cv), pipeline_mode=single),
            pl.BlockSpec((1, DF_V_DIM), lambda bi, qi: (0, 0)),
        ],
        out_specs=pl.BlockSpec((1, t, DF_V_WIDTH), lambda bi, qi: (bi, qi, 0)),
        scratch_shapes=[pltpu.VMEM((DF_HEADS, 2, t, t), F32),
                        pltpu.VMEM((DF_HEADS, 2, DF_V_DIM, t), F32),
                        pltpu.VMEM((s // t, DF_V_WIDTH, t), BF16),
                        pltpu.VMEM((2, 2 * DF_HEADS, t, t), BF16),
                        pltpu.VMEM((2, 2 * DF_HEADS, t), F32),
                        pltpu.VMEM((2 * DF_HEADS, t), F32),
                        pltpu.VMEM((2 * DF_HEADS, t), F32)],
        compiler_params=pltpu.CompilerParams(
            dimension_semantics=("arbitrary", "arbitrary"),
            vmem_limit_bytes=VMEM_LIMIT),
        name="df_attn",
    )(lam_params, proj3d, proj3d, proj3d, subln_g)


def _merge_kernel(a_ref, wa_ref, b_ref, wb_ref, ga_ref, gb_ref, o_ref):
    pa = _dot(a_ref[...], wa_ref[...])
    pb = _dot(b_ref[...], wb_ref[...])
    o_ref[...] = (ga_ref[...].astype(F32) * pa + gb_ref[...].astype(F32) * pb).astype(o_ref.dtype)


def _merge(out_a, wa, out_b, wb, proj2d, tm=1024, tn=1024):
    t = out_a.shape[0]
    grid = (t // tm, D_MODEL // tn)
    ca, cb = OFF_GA // tn, OFF_GB // tn
    return pl.pallas_call(
        _merge_kernel,
        out_shape=jax.ShapeDtypeStruct((t, D_MODEL), BF16),
        grid=grid,
        in_specs=[
            pl.BlockSpec((tm, SB_WIDTH), lambda i, j: (i, 0)),
            pl.BlockSpec((SB_WIDTH, tn), lambda i, j: (0, j)),
            pl.BlockSpec((tm, DF_V_WIDTH), lambda i, j: (i, 0)),
            pl.BlockSpec((DF_V_WIDTH, tn), lambda i, j: (0, j)),
            pl.BlockSpec((tm, tn), lambda i, j: (i, ca + j)),
            pl.BlockSpec((tm, tn), lambda i, j: (i, cb + j)),
        ],
        out_specs=pl.BlockSpec((tm, tn), lambda i, j: (i, j)),
        compiler_params=pltpu.CompilerParams(
            dimension_semantics=("arbitrary", "arbitrary"),
            vmem_limit_bytes=VMEM_LIMIT),
        name="merge",
    )(out_a, wa, out_b, wb, proj2d, proj2d)


def _out_proj_kernel(m_ref, w_ref, x_ref, g_ref, h_ref, hn_ref):
    h = x_ref[...] + _dot(m_ref[...], w_ref[...])
    h_ref[...] = h
    ms = jnp.mean(h * h, axis=-1, keepdims=True)
    hn_ref[...] = (h * lax.rsqrt(ms + EPS) * g_ref[...]).astype(hn_ref.dtype)


def _out_proj(merged, w_out, x2d, g2, tm=512):
    t = merged.shape[0]
    grid = (t // tm,)
    return pl.pallas_call(
        _out_proj_kernel,
        out_shape=(jax.ShapeDtypeStruct((t, D_MODEL), F32),
                   jax.ShapeDtypeStruct((t, D_MODEL), BF16)),
        grid=grid,
        in_specs=[
            pl.BlockSpec((tm, D_MODEL), lambda i: (i, 0)),
            pl.BlockSpec((D_MODEL, D_MODEL), lambda i: (0, 0)),
            pl.BlockSpec((tm, D_MODEL), lambda i: (i, 0)),
            pl.BlockSpec((1, D_MODEL), lambda i: (0, 0)),
        ],
        out_specs=(pl.BlockSpec((tm, D_MODEL), lambda i: (i, 0)),
                   pl.BlockSpec((tm, D_MODEL), lambda i: (i, 0))),
        compiler_params=pltpu.CompilerParams(
            dimension_semantics=("arbitrary",),
            vmem_limit_bytes=VMEM_LIMIT),
        name="out_proj",
    )(merged, w_out, x2d, g2)


FFN_SUB = 2


def _ffn_kernel(hn_ref, h_ref, wg_ref, wu_ref, wd_ref, o_ref):
    j = pl.program_id(1)

    @pl.when(j == 0)
    def _():
        o_ref[...] = h_ref[...]

    hn = hn_ref[...]
    tf = wg_ref.shape[1]
    sub = tf // FFN_SUB
    gu = []
    for c in range(FFN_SUB):
        cols = slice(c * sub, (c + 1) * sub)
        gu.append((_dot(hn, wg_ref[:, cols]), _dot(hn, wu_ref[:, cols])))
    acc = None
    for c in range(FFN_SUB):
        g, u = gu[c]
        hid = (g * jax.nn.sigmoid(g) * u).astype(BF16)
        d = _dot(hid, wd_ref[c * sub:(c + 1) * sub, :])
        acc = d if acc is None else acc + d
    o_ref[...] += acc


def _ffn(hn, h, wg, wu, wd, tm=512, tf=512):
    t = hn.shape[0]
    d_ff = wg.shape[1]
    grid = (t // tm, d_ff // tf)
    return pl.pallas_call(
        _ffn_kernel,
        out_shape=jax.ShapeDtypeStruct((t, D_MODEL), F32),
        grid=grid,
        in_specs=[
            pl.BlockSpec((tm, D_MODEL), lambda i, j: (i, 0)),
            pl.BlockSpec((tm, D_MODEL), lambda i, j: (i, 0)),
            pl.BlockSpec((D_MODEL, tf), lambda i, j: (0, j)),
            pl.BlockSpec((D_MODEL, tf), lambda i, j: (0, j)),
            pl.BlockSpec((tf, D_MODEL), lambda i, j: (j, 0)),
        ],
        out_specs=pl.BlockSpec((tm, D_MODEL), lambda i, j: (i, 0)),
        compiler_params=pltpu.CompilerParams(
            dimension_semantics=("arbitrary", "arbitrary"),
            vmem_limit_bytes=VMEM_LIMIT),
        name="ffn",
    )(hn, h, wg, wu, wd)


def kernel(x, norm1_g, w_in, q_norm_g, k_norm_g, lambda_q1, lambda_k1, lambda_q2, lambda_k2,
           subln_g, w_branch_a, w_branch_b, w_out, norm2_g, w_ffn_gate, w_ffn_up, w_ffn_down):
    b, s, d = x.shape
    t = b * s
    layer = 0
    x2d = x.reshape(t, d)
    lam_params = jnp.stack([lambda_q1[layer], lambda_k1[layer], lambda_q2[layer], lambda_k2[layer]])

    proj = _in_proj(x2d, norm1_g[layer].reshape(1, d), w_in[layer].astype(BF16),
                    q_norm_g[layer].reshape(1, HEAD_DIM), k_norm_g[layer].reshape(1, HEAD_DIM))
    proj3d = proj.reshape(b, s, IN_WIDTH)

    out_a = _sb_attn(proj3d).reshape(t, SB_WIDTH)
    out_b = _df_attn(proj3d, lam_params, subln_g[layer].reshape(1, DF_V_DIM)).reshape(t, DF_V_WIDTH)

    merged = _merge(out_a, w_branch_a[layer].astype(BF16), out_b, w_branch_b[layer].astype(BF16), proj)
    h, hn = _out_proj(merged, w_out[layer].astype(BF16), x2d, norm2_g[layer].reshape(1, d))
    out = _ffn(hn, h, w_ffn_gate[layer].astype(BF16), w_ffn_up[layer].astype(BF16),
               w_ffn_down[layer].astype(BF16))
    return out.reshape(b, s, d)
```

```python
import math

import jax
import jax.numpy as jnp
from jax import lax
from jax.experimental import pallas as pl
from jax.experimental.pallas import tpu as pltpu

F32 = jnp.float32
BF16 = jnp.bfloat16

D_MODEL = 2048
SB_HEADS = 8
HEAD_DIM = 128
DF_HEADS = 4
DF_V_DIM = 256
CHUNK = 64
SB_WIDTH = SB_HEADS * HEAD_DIM
DF_QK_WIDTH = DF_HEADS * 2 * HEAD_DIM
DF_V_WIDTH = DF_HEADS * DF_V_DIM
IN_WIDTH = 3 * SB_WIDTH + 2 * DF_QK_WIDTH + DF_V_WIDTH + 2 * D_MODEL
EPS = 1e-6
SUBLN_EPS = 1e-5
LAMBDA_INIT = 0.8 - 0.6 * math.exp(-0.3 * 0)
LOG2E = math.log2(math.e)
Q_SCALE = LOG2E / math.sqrt(HEAD_DIM)

OFF_SBQ = 0
OFF_SBK = SB_WIDTH
OFF_SBV = 2 * SB_WIDTH
OFF_DFQ = 3 * SB_WIDTH
OFF_DFK = OFF_DFQ + DF_QK_WIDTH
OFF_DFV = OFF_DFK + DF_QK_WIDTH
OFF_GA = OFF_DFV + DF_V_WIDTH
OFF_GB = OFF_GA + D_MODEL

VMEM_LIMIT = 56 * 1024 * 1024
NEG_BIG = -1e30


def _dot(a, b):
    return jnp.dot(a, b, preferred_element_type=F32)


def _dot_nt(a, b):
    return lax.dot_general(a, b, (((1,), (1,)), ((), ())), preferred_element_type=F32)


PROJ_TN = 1024


def _in_proj_kernel(x_ref, g_ref, w_ref, qg_ref, kg_ref, o_ref, xn_ref):
    j = pl.program_id(1)

    @pl.when(j == 0)
    def _():
        x = x_ref[...]
        ms = jnp.mean(x * x, axis=-1, keepdims=True)
        xn_ref[...] = (x * lax.rsqrt(ms + EPS) * g_ref[...]).astype(BF16)

    is_sbq = j == OFF_SBQ // PROJ_TN
    is_q = j == OFF_DFQ // PROJ_TN
    is_qk = is_q | (j == OFF_DFK // PROJ_TN)
    is_gate = j >= OFF_GA // PROJ_TN

    def proj():
        return _dot(xn_ref[...], w_ref[...])

    @pl.when(jnp.logical_not(is_qk | is_gate))
    def _():
        o_ref[...] = (proj() * jnp.where(is_sbq, Q_SCALE, 1.0)).astype(o_ref.dtype)

    @pl.when(is_qk)
    def _():
        acc = proj()
        gain = jnp.where(is_q, qg_ref[...] * Q_SCALE, kg_ref[...])
        for c in range(PROJ_TN // HEAD_DIM):
            a = acc[:, c * HEAD_DIM:(c + 1) * HEAD_DIM]
            ms = jnp.mean(a * a, axis=-1, keepdims=True)
            o_ref[:, c * HEAD_DIM:(c + 1) * HEAD_DIM] = (
                a * lax.rsqrt(ms + EPS) * gain).astype(o_ref.dtype)

    @pl.when(is_gate)
    def _():
        o_ref[...] = jax.nn.sigmoid(proj()).astype(o_ref.dtype)


def _in_proj(x2d, g1, w_in, qg, kg, tm=1024):
    t = x2d.shape[0]
    grid = (t // tm, IN_WIDTH // PROJ_TN)
    return pl.pallas_call(
        _in_proj_kernel,
        out_shape=jax.ShapeDtypeStruct((t, IN_WIDTH), BF16),
        grid=grid,
        in_specs=[
            pl.BlockSpec((tm, D_MODEL), lambda i, j: (i, 0)),
            pl.BlockSpec((1, D_MODEL), lambda i, j: (0, 0)),
            pl.BlockSpec((D_MODEL, PROJ_TN), lambda i, j: (0, j)),
            pl.BlockSpec((1, HEAD_DIM), lambda i, j: (0, 0)),
            pl.BlockSpec((1, HEAD_DIM), lambda i, j: (0, 0)),
        ],
        out_specs=pl.BlockSpec((tm, PROJ_TN), lambda i, j: (i, j)),
        scratch_shapes=[pltpu.VMEM((tm, D_MODEL), BF16)],
        compiler_params=pltpu.CompilerParams(
            dimension_semantics=("arbitrary", "arbitrary"),
            vmem_limit_bytes=VMEM_LIMIT),
        name="in_proj",
    )(x2d, g1, w_in, qg, kg)


SB_T = 256
SB_G = 4


def _softplus2(a):
    return jnp.maximum(a, 0.0) + jnp.log2(1.0 + jnp.exp2(jnp.minimum(a, -a)))


def _split_hi_lo(a):
    hi = a.astype(BF16)
    lo = (a - hi.astype(F32)).astype(BF16)
    return hi, lo


def _eye(n):
    return (lax.broadcasted_iota(jnp.int32, (n, n), 0)
            == lax.broadcasted_iota(jnp.int32, (n, n), 1)).astype(BF16)


def _fill_vt(v_ref, vt_ref, t):
    eye = _eye(HEAD_DIM)
    n_groups = v_ref.shape[2] // HEAD_DIM

    def body(c, _):
        start = pl.multiple_of(c * t, t)
        for j in range(n_groups):
            lanes = slice(j * HEAD_DIM, (j + 1) * HEAD_DIM)
            vt_ref[c, lanes, :] = _dot_nt(eye, v_ref[0, pl.ds(start, t), lanes]).astype(BF16)
        return 0

    lax.fori_loop(0, v_ref.shape[1] // t, body, 0)


SB_EXIT_LOG2 = 160.0
NORM_SLACK = 1.01


def _max_key_norm2(k_ref, kn_ref, t):
    ones8 = jnp.ones((8, HEAD_DIM), BF16)
    n_groups = k_ref.shape[2] // HEAD_DIM

    def body(c, mx):
        start = pl.multiple_of(c * t, t)
        out = []
        for g in range(n_groups):
            kf = k_ref[0, pl.ds(start, t), g * HEAD_DIM:(g + 1) * HEAD_DIM].astype(F32)
            out.append(jnp.maximum(mx[g], _dot_nt(ones8, (kf * kf).astype(BF16))))
        return tuple(out)

    mx = lax.fori_loop(0, k_ref.shape[1] // t, body,
                       tuple(jnp.zeros((8, t), F32) for _ in range(n_groups)))
    for g in range(n_groups):
        kn_ref[g:g + 1, :] = jnp.broadcast_to(
            jnp.max(mx[g][0:1, :], axis=1, keepdims=True), (1, HEAD_DIM))


def _sb_attn_kernel(q_ref, k_ref, v_ref, o_ref, vt_ref, a_ref, hl_ref, e_ref, kn_ref):
    qi = pl.program_id(2)
    t = SB_T
    heads = range(SB_G)

    @pl.when(qi == 0)
    def _():
        _fill_vt(v_ref, vt_ref, t)
        _max_key_norm2(k_ref, kn_ref, t)

    r2 = lax.broadcasted_iota(jnp.int32, (t, 2 * t), 0)
    c2 = lax.broadcasted_iota(jnp.int32, (t, 2 * t), 1)
    l2 = ((c2 & (t - 1)) >= r2).astype(BF16)
    lanes = [slice(g * HEAD_DIM, (g + 1) * HEAD_DIM) for g in heads]

    def block_start(j):
        return pl.multiple_of(jnp.maximum(qi - j, 0) * t, t)

    def strict_mask():
        key = lax.broadcasted_iota(jnp.int32, (t, t), 0)
        qry = lax.broadcasted_iota(jnp.int32, (t, t), 1)
        return key < qry

    def s1_matmul(j):
        start = block_start(j)
        return [_dot_nt(k_ref[0, pl.ds(start, t), lanes[g]], q_ref[0, :, lanes[g]]) for g in heads]

    def s1_finish(a, slot, masked):
        for g in heads:
            sp = _softplus2(a[g])
            if masked:
                sp = jnp.where(strict_mask(), sp, 0.0)
            hi, lo = _split_hi_lo(sp)
            a_ref[slot, g] = a[g]
            hl_ref[slot, g, 0:t, :] = hi
            hl_ref[slot, g, t:2 * t, :] = lo

    def s2_matmul(slot):
        return [_dot(l2, hl_ref[slot, g]) for g in heads]

    def s2_finish(cum, slot, masked):
        for g in heads:
            e0 = a_ref[slot, g] - cum[g]
            if masked:
                e0 = jnp.where(strict_mask(), e0, NEG_BIG)
            e_ref[slot, g] = e0
        return tuple(c[0:1, :] for c in cum)

    def s3_weights(slot, carry):
        return [jnp.exp2(e_ref[slot, g] - carry[g]).astype(BF16) for g in heads]

    def s3_matmul(j, w):
        kb = jnp.maximum(qi - j, 0)
        return [_dot(vt_ref[kb, lanes[g], :], w[g]) for g in heads]

    def step(i, slot, st):
        acc, carry, sums = st
        w = s3_weights(slot, carry)
        a = s1_matmul(i)
        cum = s2_matmul(1 - slot)
        pv = s3_matmul(i - 2, w)
        s1_finish(a, slot, False)
        new_sums = s2_finish(cum, 1 - slot, False)
        return (tuple(acc[g] + pv[g] for g in heads),
                tuple(carry[g] + sums[g] for g in heads), new_sums)

    ones8 = jnp.ones((8, HEAD_DIM), BF16)
    bound = []
    for g in heads:
        qf = q_ref[0, :, lanes[g]].astype(F32)
        qn2 = _dot_nt(ones8, (qf * qf).astype(BF16))[0:1, :]
        bound.append(jnp.sqrt(qn2 * kn_ref[g:g + 1, 0:1]) * NORM_SLACK)

    def all_underflow(carry):
        slack = bound[0] - carry[0]
        for g in heads[1:]:
            slack = jnp.maximum(slack, bound[g] - carry[g])
        return jnp.max(slack) < -SB_EXIT_LOG2

    a0 = s1_matmul(0)
    a1 = s1_matmul(1)
    s1_finish(a0, 0, True)
    s1_finish(a1, 1, False)
    cum0 = s2_matmul(0)
    cum1 = s2_matmul(1)
    sums0 = s2_finish(cum0, 0, True)
    sums1 = s2_finish(cum1, 1, False)
    w0 = s3_weights(0, tuple(jnp.zeros((1, t), F32) for _ in heads))
    w1 = s3_weights(1, sums0)
    pv0 = s3_matmul(0, w0)
    pv1 = s3_matmul(1, w1)
    acc2 = tuple(pv0[g] + jnp.where(qi > 0, pv1[g], 0.0) for g in heads)
    carry2 = tuple(sums0[g] + sums1[g] for g in heads)

    def remaining_blocks(acc, carry):
        s1_finish(s1_matmul(2), 0, False)
        a = s1_matmul(3)
        cum = s2_matmul(0)
        s1_finish(a, 1, False)
        sums = s2_finish(cum, 0, False)
        pairs = (qi - 2) // 2

        def cond(st):
            k, done = st[0], st[1]
            return (k < pairs) & jnp.logical_not(done)

        def body(st):
            k = st[0]
            acc, carry, sums = step(2 * k + 5, 1, step(2 * k + 4, 0, st[2:]))
            return (k + 1, all_underflow(carry), acc, carry, sums)

        k, _, acc, carry, sums = lax.while_loop(
            cond, body, (jnp.int32(0), False, acc, carry, sums))

        w = s3_weights(0, carry)
        cum = s2_matmul(1)
        pv = s3_matmul(2 * k + 2, w)
        s2_finish(cum, 1, False)
        acc = tuple(acc[g] + pv[g] for g in heads)
        carry = tuple(carry[g] + sums[g] for g in heads)
        pv = s3_matmul(2 * k + 3, s3_weights(1, carry))
        last_is_real = 2 * k + 3 <= qi
        return tuple(acc[g] + jnp.where(last_is_real, pv[g], 0.0) for g in heads)

    more = (qi >= 2) & jnp.logical_not(all_underflow(carry2))
    acc = lax.cond(more, lambda: remaining_blocks(acc2, carry2), lambda: acc2)
    for g in heads:
        o_ref[0, :, lanes[g]] = acc[g].T.astype(o_ref.dtype)


def _sb_attn(proj3d):
    b, s, _ = proj3d.shape
    t = SB_T
    w = SB_G * HEAD_DIM
    grid = (b, SB_HEADS // SB_G, s // t)
    cq, ck, cv = OFF_SBQ // w, OFF_SBK // w, OFF_SBV // w
    single = pl.Buffered(1)
    return pl.pallas_call(
        _sb_attn_kernel,
        out_shape=jax.ShapeDtypeStruct((b, s, SB_WIDTH), BF16),
        grid=grid,
        in_specs=[
            pl.BlockSpec((1, t, w), lambda bi, h, qi: (bi, qi, cq + h)),
            pl.BlockSpec((1, s, w), lambda bi, h, qi: (bi, 0, ck + h), pipeline_mode=single),
            pl.BlockSpec((1, s, w), lambda bi, h, qi: (bi, 0, cv + h), pipeline_mode=single),
        ],
        out_specs=pl.BlockSpec((1, t, w), lambda bi, h, qi: (bi, qi, h)),
        scratch_shapes=[pltpu.VMEM((s // t, w, t), BF16),
                        pltpu.VMEM((2, SB_G, t, t), F32),
                        pltpu.VMEM((2, SB_G, 2 * t, t), BF16),
                        pltpu.VMEM((2, SB_G, t, t), F32),
                        pltpu.VMEM((SB_G, HEAD_DIM), F32)],
        compiler_params=pltpu.CompilerParams(
            dimension_semantics=("arbitrary", "arbitrary", "arbitrary"),
            vmem_limit_bytes=VMEM_LIMIT),
        name="sb_attn",
    )(proj3d, proj3d, proj3d)


DF_T = 256
DF_SLOPES = [2.0 ** (-8.0 * (h + 1) / DF_HEADS) for h in range(DF_HEADS)]


def _df_attn_kernel(lamp_ref, q_ref, k_ref, v_ref, sg_ref, wg_ref, wu_ref, wd_ref,
                    o_ref, wg_out_ref, wu_out_ref, wd_out_ref,
                    bias_ref, acc_ref, vt_ref, p_ref, al_ref, m_ref, l_ref):
    bi = pl.program_id(0)
    qi = pl.program_id(1)
    t = DF_T

    wg_out_ref[...] = wg_ref[...].astype(BF16)
    wu_out_ref[...] = wu_ref[...].astype(BF16)

    @pl.when((qi & 1) == 0)
    def _():
        wd_out_ref[...] = wd_ref[...].astype(BF16)

    @pl.when(qi == 0)
    def _():
        _fill_vt(v_ref, vt_ref, t)

    @pl.when((bi == 0) & (qi == 0))
    def _():
        key = lax.broadcasted_iota(jnp.int32, (t, t), 0)
        qry = lax.broadcasted_iota(jnp.int32, (t, t), 1)
        rel = (qry - key).astype(F32)
        allowed = (key // CHUNK) <= (qry // CHUNK)
        for h in range(DF_HEADS):
            sl = DF_SLOPES[h] * LOG2E
            bias_ref[h, 0] = -sl * rel
            bias_ref[h, 1] = jnp.where(allowed, -sl * jnp.abs(rel), NEG_BIG)

    chains = [(h, c) for h in range(DF_HEADS) for c in range(2)]

    def s1_matmul(j):
        start = pl.multiple_of(j * t, t)
        out = []
        for h, c in chains:
            lanes = slice((2 * h + c) * HEAD_DIM, (2 * h + c + 1) * HEAD_DIM)
            out.append(_dot_nt(k_ref[0, pl.ds(start, t), lanes], q_ref[0, :, lanes]))
        return out

    def s1_finish(a, j, slot):
        which = (j == qi).astype(jnp.int32)
        dist = ((qi - j) * t).astype(F32)
        for i, (h, c) in enumerate(chains):
            row = slice(i, i + 1)
            shift = -(DF_SLOPES[h] * LOG2E) * dist
            s = a[i] + bias_ref[h, which]
            m = m_ref[row, :]
            m_new = jnp.maximum(m, jnp.max(s, axis=0, keepdims=True) + shift)
            al = jnp.exp2(m - m_new)
            p = jnp.exp2(s - (m_new - shift))
            m_ref[row, :] = m_new
            l_ref[row, :] = al * l_ref[row, :] + jnp.sum(p, axis=0, keepdims=True)
            al_ref[slot, row, :] = al
            p_ref[slot, i] = p.astype(BF16)

    def s2_matmul(j, slot):
        return [_dot(vt_ref[j, h * DF_V_DIM:(h + 1) * DF_V_DIM, :], p_ref[slot, i])
                for i, (h, c) in enumerate(chains)]

    def s2_finish(pv, slot):
        for i, (h, c) in enumerate(chains):
            acc_ref[h, c] = al_ref[slot, i:i + 1, :] * acc_ref[h, c] + pv[i]

    def step(j, slot):
        a = s1_matmul(j)
        pv = s2_matmul(j - 1, 1 - slot)
        s2_finish(pv, 1 - slot)
        s1_finish(a, j, slot)

    acc_ref[...] = jnp.zeros_like(acc_ref)
    m_ref[...] = jnp.full_like(m_ref, NEG_BIG)
    l_ref[...] = jnp.zeros_like(l_ref)
    s1_finish(s1_matmul(0), 0, 0)

    def body(k, carry):
        step(2 * k + 1, 1)
        step(2 * k + 2, 0)
        return carry

    lax.fori_loop(0, qi // 2, body, 0)

    @pl.when((qi & 1) == 1)
    def _():
        step(qi, 1)
        s2_finish(s2_matmul(qi, 1), 1)

    @pl.when((qi & 1) == 0)
    def _():
        s2_finish(s2_matmul(qi, 0), 0)

    lp = lamp_ref[...]
    lam = (jnp.exp(jnp.sum(lp[0:1] * lp[1:2], axis=-1, keepdims=True))
           - jnp.exp(jnp.sum(lp[2:3] * lp[3:4], axis=-1, keepdims=True)) + LAMBDA_INIT)
    for h in range(DF_HEADS):
        l1, l2 = l_ref[2 * h:2 * h + 1, :], l_ref[2 * h + 1:2 * h + 2, :]
        out_t = acc_ref[h, 0] / l1 - lam * (acc_ref[h, 1] / l2)
        ms = jnp.mean(out_t * out_t, axis=0, keepdims=True)
        out = (out_t * lax.rsqrt(ms + SUBLN_EPS)).T * (sg_ref[...] * (1.0 - LAMBDA_INIT))
        o_ref[0, :, h * DF_V_DIM:(h + 1) * DF_V_DIM] = out.astype(o_ref.dtype)


def _df_attn(proj3d, lam_params, subln_g, wg, wu, wd):
    b, s, _ = proj3d.shape
    t = DF_T
    nq = s // t
    grid = (b, nq)
    steps = b * nq
    cq, ck, cv = OFF_DFQ // DF_QK_WIDTH, OFF_DFK // DF_QK_WIDTH, OFF_DFV // DF_V_WIDTH
    single = pl.Buffered(1)
    d, d_ff = wg.shape
    wg3 = wg.reshape(steps, d // steps, d_ff)
    wu3 = wu.reshape(steps, d // steps, d_ff)
    wd3 = wd.reshape(steps // 2, 2 * d_ff // steps, d)
    gu_spec = pl.BlockSpec((1,) + wg3.shape[1:], lambda bi, qi: (bi * nq + qi, 0, 0))
    dn_spec = pl.BlockSpec((1,) + wd3.shape[1:], lambda bi, qi: ((bi * nq + qi) // 2, 0, 0))
    out_b, wg_bf, wu_bf, wd_bf = pl.pallas_call(
        _df_attn_kernel,
        out_shape=(jax.ShapeDtypeStruct((b, s, DF_V_WIDTH), BF16),
                   jax.ShapeDtypeStruct(wg3.shape, BF16),
                   jax.ShapeDtypeStruct(wu3.shape, BF16),
                   jax.ShapeDtypeStruct(wd3.shape, BF16)),
        grid=grid,
        in_specs=[
            pl.BlockSpec((4, HEAD_DIM), lambda bi, qi: (0, 0)),
            pl.BlockSpec((1, t, DF_QK_WIDTH), lambda bi, qi: (bi, qi, cq)),
            pl.BlockSpec((1, s, DF_QK_WIDTH), lambda bi, qi: (bi, 0, ck), pipeline_mode=single),
            pl.BlockSpec((1, s, DF_V_WIDTH), lambda bi, qi: (bi, 0, cv), pipeline_mode=single),
            pl.BlockSpec((1, DF_V_DIM), lambda bi, qi: (0, 0)),
            gu_spec, gu_spec, dn_spec,
        ],
        out_specs=(pl.BlockSpec((1, t, DF_V_WIDTH), lambda bi, qi: (bi, qi, 0)),
                   gu_spec, gu_spec, dn_spec),
        scratch_shapes=[pltpu.VMEM((DF_HEADS, 2, t, t), F32),
                        pltpu.VMEM((DF_HEADS, 2, DF_V_DIM, t), F32),
                        pltpu.VMEM((s // t, DF_V_WIDTH, t), BF16),
                        pltpu.VMEM((2, 2 * DF_HEADS, t, t), BF16),
                        pltpu.VMEM((2, 2 * DF_HEADS, t), F32),
                        pltpu.VMEM((2 * DF_HEADS, t), F32),
                        pltpu.VMEM((2 * DF_HEADS, t), F32)],
        compiler_params=pltpu.CompilerParams(
            dimension_semantics=("arbitrary", "arbitrary"),
            vmem_limit_bytes=VMEM_LIMIT),
        name="df_attn",
    )(lam_params, proj3d, proj3d, proj3d, subln_g, wg3, wu3, wd3)
    return out_b, wg_bf.reshape(wg.shape), wu_bf.reshape(wu.shape), wd_bf.reshape(wd.shape)


def _merge_kernel(a_ref, wa_ref, b_ref, wb_ref, ga_ref, gb_ref, o_ref):
    pa = _dot(a_ref[...], wa_ref[...])
    pb = _dot(b_ref[...], wb_ref[...])
    o_ref[...] = (ga_ref[...].astype(F32) * pa + gb_ref[...].astype(F32) * pb).astype(o_ref.dtype)


def _merge(out_a, wa, out_b, wb, proj2d, tm=1024, tn=1024):
    t = out_a.shape[0]
    grid = (t // tm, D_MODEL // tn)
    ca, cb = OFF_GA // tn, OFF_GB // tn
    return pl.pallas_call(
        _merge_kernel,
        out_shape=jax.ShapeDtypeStruct((t, D_MODEL), BF16),
        grid=grid,
        in_specs=[
            pl.BlockSpec((tm, SB_WIDTH), lambda i, j: (i, 0)),
            pl.BlockSpec((SB_WIDTH, tn), lambda i, j: (0, j)),
            pl.BlockSpec((tm, DF_V_WIDTH), lambda i, j: (i, 0)),
            pl.BlockSpec((DF_V_WIDTH, tn), lambda i, j: (0, j)),
            pl.BlockSpec((tm, tn), lambda i, j: (i, ca + j)),
            pl.BlockSpec((tm, tn), lambda i, j: (i, cb + j)),
        ],
        out_specs=pl.BlockSpec((tm, tn), lambda i, j: (i, j)),
        compiler_params=pltpu.CompilerParams(
            dimension_semantics=("arbitrary", "arbitrary"),
            vmem_limit_bytes=VMEM_LIMIT),
        name="merge",
    )(out_a, wa, out_b, wb, proj2d, proj2d)


def _out_proj_kernel(m_ref, w_ref, x_ref, g_ref, h_ref, hn_ref):
    h = x_ref[...] + _dot(m_ref[...], w_ref[...])
    h_ref[...] = h
    ms = jnp.mean(h * h, axis=-1, keepdims=True)
    hn_ref[...] = (h * lax.rsqrt(ms + EPS) * g_ref[...]).astype(hn_ref.dtype)


def _out_proj(merged, w_out, x2d, g2, tm=512):
    t = merged.shape[0]
    grid = (t // tm,)
    return pl.pallas_call(
        _out_proj_kernel,
        out_shape=(jax.ShapeDtypeStruct((t, D_MODEL), F32),
                   jax.ShapeDtypeStruct((t, D_MODEL), BF16)),
        grid=grid,
        in_specs=[
            pl.BlockSpec((tm, D_MODEL), lambda i: (i, 0)),
            pl.BlockSpec((D_MODEL, D_MODEL), lambda i: (0, 0)),
            pl.BlockSpec((tm, D_MODEL), lambda i: (i, 0)),
            pl.BlockSpec((1, D_MODEL), lambda i: (0, 0)),
        ],
        out_specs=(pl.BlockSpec((tm, D_MODEL), lambda i: (i, 0)),
                   pl.BlockSpec((tm, D_MODEL), lambda i: (i, 0))),
        compiler_params=pltpu.CompilerParams(
            dimension_semantics=("arbitrary",),
            vmem_limit_bytes=VMEM_LIMIT),
        name="out_proj",
    )(merged, w_out, x2d, g2)


FFN_SUB = 2


def _ffn_kernel(hn_ref, h_ref, wg_ref, wu_ref, wd_ref, o_ref):
    j = pl.program_id(1)

    @pl.when(j == 0)
    def _():
        o_ref[...] = h_ref[...]

    hn = hn_ref[...]
    tf = wg_ref.shape[1]
    sub = tf // FFN_SUB
    gu = []
    for c in range(FFN_SUB):
        cols = slice(c * sub, (c + 1) * sub)
        gu.append((_dot(hn, wg_ref[:, cols]), _dot(hn, wu_ref[:, cols])))
    acc = None
    for c in range(FFN_SUB):
        g, u = gu[c]
        hid = (g * jax.nn.sigmoid(g) * u).astype(BF16)
        d = _dot(hid, wd_ref[c * sub:(c + 1) * sub, :])
        acc = d if acc is None else acc + d
    o_ref[...] += acc


def _ffn(hn, h, wg, wu, wd, tm=1024, tf=512):
    t = hn.shape[0]
    d_ff = wg.shape[1]
    grid = (t // tm, d_ff // tf)
    return pl.pallas_call(
        _ffn_kernel,
        out_shape=jax.ShapeDtypeStruct((t, D_MODEL), F32),
        grid=grid,
        in_specs=[
            pl.BlockSpec((tm, D_MODEL), lambda i, j: (i, 0)),
            pl.BlockSpec((tm, D_MODEL), lambda i, j: (i, 0), pipeline_mode=pl.Buffered(1)),
            pl.BlockSpec((D_MODEL, tf), lambda i, j: (0, j)),
            pl.BlockSpec((D_MODEL, tf), lambda i, j: (0, j)),
            pl.BlockSpec((tf, D_MODEL), lambda i, j: (j, 0)),
        ],
        out_specs=pl.BlockSpec((tm, D_MODEL), lambda i, j: (i, 0)),
        compiler_params=pltpu.CompilerParams(
            dimension_semantics=("arbitrary", "arbitrary"),
            vmem_limit_bytes=VMEM_LIMIT),
        name="ffn",
    )(hn, h, wg, wu, wd)


def kernel(x, norm1_g, w_in, q_norm_g, k_norm_g, lambda_q1, lambda_k1, lambda_q2, lambda_k2,
           subln_g, w_branch_a, w_branch_b, w_out, norm2_g, w_ffn_gate, w_ffn_up, w_ffn_down):
    b, s, d = x.shape
    t = b * s
    layer = 0
    x2d = x.reshape(t, d)
    lam_params = jnp.stack([lambda_q1[layer], lambda_k1[layer], lambda_q2[layer], lambda_k2[layer]])

    proj = _in_proj(x2d, norm1_g[layer].reshape(1, d), w_in[layer].astype(BF16),
                    q_norm_g[layer].reshape(1, HEAD_DIM), k_norm_g[layer].reshape(1, HEAD_DIM))
    proj3d = proj.reshape(b, s, IN_WIDTH)

    out_a = _sb_attn(proj3d).reshape(t, SB_WIDTH)
    out_b, wg, wu, wd = _df_attn(proj3d, lam_params, subln_g[layer].reshape(1, DF_V_DIM),
                                 w_ffn_gate[layer], w_ffn_up[layer], w_ffn_down[layer])
    out_b = out_b.reshape(t, DF_V_WIDTH)

    merged = _merge(out_a, w_branch_a[layer].astype(BF16), out_b, w_branch_b[layer].astype(BF16), proj)
    h, hn = _out_proj(merged, w_out[layer].astype(BF16), x2d, norm2_g[layer].reshape(1, d))
    out = _ffn(hn, h, wg, wu, wd)
    return out.reshape(b, s, d)
```

```python
import math

import numpy as np
import jax
import jax.numpy as jnp
from jax import lax
from jax.experimental import pallas as pl
from jax.experimental.pallas import tpu as pltpu

F32 = jnp.float32
BF16 = jnp.bfloat16

D_MODEL = 2048
SB_HEADS = 8
HEAD_DIM = 128
DF_HEADS = 4
DF_V_DIM = 256
CHUNK = 64
SB_WIDTH = SB_HEADS * HEAD_DIM
DF_QK_WIDTH = DF_HEADS * 2 * HEAD_DIM
DF_V_WIDTH = DF_HEADS * DF_V_DIM
IN_WIDTH = 3 * SB_WIDTH + 2 * DF_QK_WIDTH + DF_V_WIDTH + 2 * D_MODEL
EPS = 1e-6
SUBLN_EPS = 1e-5
LAMBDA_INIT = 0.8 - 0.6 * math.exp(-0.3 * 0)
LOG2E = math.log2(math.e)
Q_SCALE = LOG2E / math.sqrt(HEAD_DIM)

OFF_SBQ = 0
OFF_SBK = SB_WIDTH
OFF_SBV = 2 * SB_WIDTH
OFF_DFQ = 3 * SB_WIDTH
OFF_DFK = OFF_DFQ + DF_QK_WIDTH
OFF_DFV = OFF_DFK + DF_QK_WIDTH
OFF_GA = OFF_DFV + DF_V_WIDTH
OFF_GB = OFF_GA + D_MODEL

VMEM_LIMIT = 56 * 1024 * 1024
NEG_BIG = -1e30


def _dot(a, b):
    return jnp.dot(a, b, preferred_element_type=F32)


def _dot_nt(a, b):
    return lax.dot_general(a, b, (((1,), (1,)), ((), ())), preferred_element_type=F32)


PROJ_TN = 1024


def _in_proj_kernel(x_ref, g_ref, w_ref, qg_ref, kg_ref, o_ref, xn_ref):
    j = pl.program_id(1)

    @pl.when(j == 0)
    def _():
        x = x_ref[...]
        ms = jnp.mean(x * x, axis=-1, keepdims=True)
        xn_ref[...] = (x * lax.rsqrt(ms + EPS) * g_ref[...]).astype(BF16)

    is_sbq = j == OFF_SBQ // PROJ_TN
    is_q = j == OFF_DFQ // PROJ_TN
    is_qk = is_q | (j == OFF_DFK // PROJ_TN)
    is_gate = j >= OFF_GA // PROJ_TN

    def proj():
        return _dot(xn_ref[...], w_ref[...])

    @pl.when(jnp.logical_not(is_qk | is_gate))
    def _():
        o_ref[...] = (proj() * jnp.where(is_sbq, Q_SCALE, 1.0)).astype(o_ref.dtype)

    @pl.when(is_qk)
    def _():
        acc = proj()
        gain = jnp.where(is_q, qg_ref[...] * Q_SCALE, kg_ref[...])
        for c in range(PROJ_TN // HEAD_DIM):
            a = acc[:, c * HEAD_DIM:(c + 1) * HEAD_DIM]
            ms = jnp.mean(a * a, axis=-1, keepdims=True)
            o_ref[:, c * HEAD_DIM:(c + 1) * HEAD_DIM] = (
                a * lax.rsqrt(ms + EPS) * gain).astype(o_ref.dtype)

    @pl.when(is_gate)
    def _():
        o_ref[...] = jax.nn.sigmoid(proj()).astype(o_ref.dtype)


def _in_proj(x2d, g1, w_in, qg, kg, tm=1024):
    t = x2d.shape[0]
    grid = (t // tm, IN_WIDTH // PROJ_TN)
    return pl.pallas_call(
        _in_proj_kernel,
        out_shape=jax.ShapeDtypeStruct((t, IN_WIDTH), BF16),
        grid=grid,
        in_specs=[
            pl.BlockSpec((tm, D_MODEL), lambda i, j: (i, 0)),
            pl.BlockSpec((1, D_MODEL), lambda i, j: (0, 0)),
            pl.BlockSpec((D_MODEL, PROJ_TN), lambda i, j: (0, j)),
            pl.BlockSpec((1, HEAD_DIM), lambda i, j: (0, 0)),
            pl.BlockSpec((1, HEAD_DIM), lambda i, j: (0, 0)),
        ],
        out_specs=pl.BlockSpec((tm, PROJ_TN), lambda i, j: (i, j)),
        scratch_shapes=[pltpu.VMEM((tm, D_MODEL), BF16)],
        compiler_params=pltpu.CompilerParams(
            dimension_semantics=("arbitrary", "arbitrary"),
            vmem_limit_bytes=VMEM_LIMIT),
        name="in_proj",
    )(x2d, g1, w_in, qg, kg)


SB_T = 256
SB_G = 4


def _softplus2(a):
    return jnp.maximum(a, 0.0) + jnp.log2(1.0 + jnp.exp2(jnp.minimum(a, -a)))


def _split_hi_lo(a):
    hi = a.astype(BF16)
    lo = (a - hi.astype(F32)).astype(BF16)
    return hi, lo


def _eye(n):
    return (lax.broadcasted_iota(jnp.int32, (n, n), 0)
            == lax.broadcasted_iota(jnp.int32, (n, n), 1)).astype(BF16)


def _fill_vt(v_ref, vt_ref, t, head_dim=HEAD_DIM, head_rows=HEAD_DIM):
    eye = _eye(HEAD_DIM)
    n_heads = v_ref.shape[2] // head_dim

    def body(c, _):
        start = pl.multiple_of(c * t, t)
        for h in range(n_heads):
            for g in range(head_dim // HEAD_DIM):
                lanes = slice(h * head_dim + g * HEAD_DIM, h * head_dim + (g + 1) * HEAD_DIM)
                rows = slice(h * head_rows + g * HEAD_DIM, h * head_rows + (g + 1) * HEAD_DIM)
                vt_ref[c, rows, :] = _dot_nt(eye, v_ref[0, pl.ds(start, t), lanes]).astype(BF16)
            if head_rows > head_dim:
                vt_ref[c, h * head_rows + head_dim:(h + 1) * head_rows, :] = jnp.ones(
                    (head_rows - head_dim, t), BF16)
        return 0

    lax.fori_loop(0, v_ref.shape[1] // t, body, 0)


SB_EXIT_LOG2 = 160.0
NORM_SLACK = 1.01


def _max_key_norm2(k_ref, kn_ref, t):
    ones8 = jnp.ones((8, HEAD_DIM), BF16)
    n_groups = k_ref.shape[2] // HEAD_DIM

    def body(c, mx):
        start = pl.multiple_of(c * t, t)
        out = []
        for g in range(n_groups):
            kf = k_ref[0, pl.ds(start, t), g * HEAD_DIM:(g + 1) * HEAD_DIM].astype(F32)
            out.append(jnp.maximum(mx[g], _dot_nt(ones8, (kf * kf).astype(BF16))))
        return tuple(out)

    mx = lax.fori_loop(0, k_ref.shape[1] // t, body,
                       tuple(jnp.zeros((8, t), F32) for _ in range(n_groups)))
    for g in range(n_groups):
        kn_ref[g:g + 1, :] = jnp.broadcast_to(
            jnp.max(mx[g][0:1, :], axis=1, keepdims=True), (1, HEAD_DIM))


def _sb_attn_kernel(q_ref, k_ref, v_ref, o_ref, vt_ref, a_ref, hl_ref, e_ref, kn_ref):
    qi = pl.program_id(2)
    t = SB_T
    heads = range(SB_G)

    @pl.when(qi == 0)
    def _():
        _fill_vt(v_ref, vt_ref, t)
        _max_key_norm2(k_ref, kn_ref, t)

    r2 = lax.broadcasted_iota(jnp.int32, (t, 2 * t), 0)
    c2 = lax.broadcasted_iota(jnp.int32, (t, 2 * t), 1)
    l2 = ((c2 & (t - 1)) >= r2).astype(BF16)
    lanes = [slice(g * HEAD_DIM, (g + 1) * HEAD_DIM) for g in heads]

    def block_start(j):
        return pl.multiple_of(jnp.maximum(qi - j, 0) * t, t)

    def strict_mask():
        key = lax.broadcasted_iota(jnp.int32, (t, t), 0)
        qry = lax.broadcasted_iota(jnp.int32, (t, t), 1)
        return key < qry

    def s1_matmul(j):
        start = block_start(j)
        return [_dot_nt(k_ref[0, pl.ds(start, t), lanes[g]], q_ref[0, :, lanes[g]]) for g in heads]

    def s1_finish(a, slot, masked):
        for g in heads:
            sp = _softplus2(a[g])
            if masked:
                sp = jnp.where(strict_mask(), sp, 0.0)
            hi, lo = _split_hi_lo(sp)
            a_ref[slot, g] = a[g]
            hl_ref[slot, g, 0:t, :] = hi
            hl_ref[slot, g, t:2 * t, :] = lo

    def s2_matmul(slot):
        return [_dot(l2, hl_ref[slot, g]) for g in heads]

    def s2_finish(cum, slot, masked):
        for g in heads:
            e0 = a_ref[slot, g] - cum[g]
            if masked:
                e0 = jnp.where(strict_mask(), e0, NEG_BIG)
            e_ref[slot, g] = e0
        return tuple(c[0:1, :] for c in cum)

    def s3_weights(slot, carry):
        return [jnp.exp2(e_ref[slot, g] - carry[g]).astype(BF16) for g in heads]

    def s3_matmul(j, w):
        kb = jnp.maximum(qi - j, 0)
        return [_dot(vt_ref[kb, lanes[g], :], w[g]) for g in heads]

    def step(i, slot, st):
        acc, carry, sums = st
        w = s3_weights(slot, carry)
        a = s1_matmul(i)
        cum = s2_matmul(1 - slot)
        pv = s3_matmul(i - 2, w)
        s1_finish(a, slot, False)
        new_sums = s2_finish(cum, 1 - slot, False)
        return (tuple(acc[g] + pv[g] for g in heads),
                tuple(carry[g] + sums[g] for g in heads), new_sums)

    ones8 = jnp.ones((8, HEAD_DIM), BF16)
    bound = []
    for g in heads:
        qf = q_ref[0, :, lanes[g]].astype(F32)
        qn2 = _dot_nt(ones8, (qf * qf).astype(BF16))[0:1, :]
        bound.append(jnp.sqrt(qn2 * kn_ref[g:g + 1, 0:1]) * NORM_SLACK)

    def all_underflow(carry):
        slack = bound[0] - carry[0]
        for g in heads[1:]:
            slack = jnp.maximum(slack, bound[g] - carry[g])
        return jnp.max(slack) < -SB_EXIT_LOG2

    a0 = s1_matmul(0)
    a1 = s1_matmul(1)
    s1_finish(a0, 0, True)
    s1_finish(a1, 1, False)
    cum0 = s2_matmul(0)
    cum1 = s2_matmul(1)
    sums0 = s2_finish(cum0, 0, True)
    sums1 = s2_finish(cum1, 1, False)
    w0 = s3_weights(0, tuple(jnp.zeros((1, t), F32) for _ in heads))
    w1 = s3_weights(1, sums0)
    pv0 = s3_matmul(0, w0)
    pv1 = s3_matmul(1, w1)
    acc2 = tuple(pv0[g] + jnp.where(qi > 0, pv1[g], 0.0) for g in heads)
    carry2 = tuple(sums0[g] + sums1[g] for g in heads)

    def remaining_blocks(acc, carry):
        s1_finish(s1_matmul(2), 0, False)
        a = s1_matmul(3)
        cum = s2_matmul(0)
        s1_finish(a, 1, False)
        sums = s2_finish(cum, 0, False)
        pairs = (qi - 2) // 2

        def cond(st):
            k, done = st[0], st[1]
            return (k < pairs) & jnp.logical_not(done)

        def body(st):
            k = st[0]
            acc, carry, sums = step(2 * k + 5, 1, step(2 * k + 4, 0, st[2:]))
            return (k + 1, all_underflow(carry), acc, carry, sums)

        k, _, acc, carry, sums = lax.while_loop(
            cond, body, (jnp.int32(0), False, acc, carry, sums))

        w = s3_weights(0, carry)
        cum = s2_matmul(1)
        pv = s3_matmul(2 * k + 2, w)
        s2_finish(cum, 1, False)
        acc = tuple(acc[g] + pv[g] for g in heads)
        carry = tuple(carry[g] + sums[g] for g in heads)
        pv = s3_matmul(2 * k + 3, s3_weights(1, carry))
        last_is_real = 2 * k + 3 <= qi
        return tuple(acc[g] + jnp.where(last_is_real, pv[g], 0.0) for g in heads)

    more = (qi >= 2) & jnp.logical_not(all_underflow(carry2))
    acc = lax.cond(more, lambda: remaining_blocks(acc2, carry2), lambda: acc2)
    for g in heads:
        o_ref[0, :, lanes[g]] = acc[g].T.astype(o_ref.dtype)


def _sb_attn(proj3d):
    b, s, _ = proj3d.shape
    t = SB_T
    w = SB_G * HEAD_DIM
    grid = (b, SB_HEADS // SB_G, s // t)
    cq, ck, cv = OFF_SBQ // w, OFF_SBK // w, OFF_SBV // w
    single = pl.Buffered(1)
    return pl.pallas_call(
        _sb_attn_kernel,
        out_shape=jax.ShapeDtypeStruct((b, s, SB_WIDTH), BF16),
        grid=grid,
        in_specs=[
            pl.BlockSpec((1, t, w), lambda bi, h, qi: (bi, qi, cq + h)),
            pl.BlockSpec((1, s, w), lambda bi, h, qi: (bi, 0, ck + h), pipeline_mode=single),
            pl.BlockSpec((1, s, w), lambda bi, h, qi: (bi, 0, cv + h), pipeline_mode=single),
        ],
        out_specs=pl.BlockSpec((1, t, w), lambda bi, h, qi: (bi, qi, h)),
        scratch_shapes=[pltpu.VMEM((s // t, w, t), BF16),
                        pltpu.VMEM((2, SB_G, t, t), F32),
                        pltpu.VMEM((2, SB_G, 2 * t, t), BF16),
                        pltpu.VMEM((2, SB_G, t, t), F32),
                        pltpu.VMEM((SB_G, HEAD_DIM), F32)],
        compiler_params=pltpu.CompilerParams(
            dimension_semantics=("arbitrary", "arbitrary", "arbitrary"),
            vmem_limit_bytes=VMEM_LIMIT),
        name="sb_attn",
    )(proj3d, proj3d, proj3d)


DF_T = 256
DF_SLOPES = [2.0 ** (-8.0 * (h + 1) / DF_HEADS) for h in range(DF_HEADS)]
DF_V_AUG = DF_V_DIM + 16
DF_SLOPE_PIECES = 4


def _bf16_pieces(x, n):
    pieces = []
    for _ in range(n):
        p = float(np.asarray(x, dtype=np.float32).astype(jnp.bfloat16).astype(np.float32))
        pieces.append(p)
        x = x - p
    return pieces


def _df_attn_kernel(lamp_ref, q_ref, k_ref, v_ref, sg_ref, wg_ref, wu_ref, wd_ref,
                    o_ref, wg_out_ref, wu_out_ref, wd_out_ref,
                    bias_ref, kc_ref, qc_ref, acc_ref, vt_ref, p_ref, al_ref, m_ref):
    bi = pl.program_id(0)
    qi = pl.program_id(1)
    t = DF_T
    va = DF_V_AUG

    wg_out_ref[...] = wg_ref[...].astype(BF16)
    wu_out_ref[...] = wu_ref[...].astype(BF16)

    @pl.when((qi & 1) == 0)
    def _():
        wd_out_ref[...] = wd_ref[...].astype(BF16)

    @pl.when(qi == 0)
    def _():
        _fill_vt(v_ref, vt_ref, t, DF_V_DIM, va)

    @pl.when((bi == 0) & (qi == 0))
    def _():
        key = lax.broadcasted_iota(jnp.int32, (t, t), 0)
        qry = lax.broadcasted_iota(jnp.int32, (t, t), 1)
        rel = (qry - key).astype(F32)
        allowed = (key // CHUNK) <= (qry // CHUNK)
        pos = lax.broadcasted_iota(jnp.int32, (t, HEAD_DIM), 0).astype(F32)
        col = lax.broadcasted_iota(jnp.int32, (t, HEAD_DIM), 1)
        n = DF_SLOPE_PIECES
        for h in range(DF_HEADS):
            sl = DF_SLOPES[h] * LOG2E
            bias_ref[h] = jnp.where(allowed, -sl * jnp.abs(rel), NEG_BIG)
            pieces = jnp.zeros((t, HEAD_DIM), F32)
            for idx, piece in enumerate(_bf16_pieces(sl, n)):
                pieces = jnp.where((col == idx) | (col == n + idx), piece, pieces)
            kc_ref[h] = jnp.where(col < n, pos, jnp.where(col < 2 * n, pieces, 0.0)).astype(BF16)
            qc_ref[h] = jnp.where(col < n, pieces, jnp.where(col < 2 * n, -pos, 0.0)).astype(BF16)

    chains = [(h, c) for h in range(DF_HEADS) for c in range(2)]

    def key_block(j):
        return jnp.where(j == 0, qi, j - 1)

    def s1_matmul(j, diagonal):
        start = pl.multiple_of(key_block(j) * t, t)
        out = []
        for h, c in chains:
            lanes = slice((2 * h + c) * HEAD_DIM, (2 * h + c + 1) * HEAD_DIM)
            kb, qb = k_ref[0, pl.ds(start, t), lanes], q_ref[0, :, lanes]
            if diagonal:
                out.append(_dot_nt(kb, qb) + bias_ref[h])
            else:
                out.append(_dot_nt(jnp.concatenate([kb, kc_ref[h]], axis=1),
                                   jnp.concatenate([qb, qc_ref[h]], axis=1)))
        return out

    def s1_finish(s, j, slot):
        dist = ((qi - key_block(j)) * t).astype(F32)
        for i, (h, c) in enumerate(chains):
            row = slice(i, i + 1)
            shift = -(DF_SLOPES[h] * LOG2E) * dist
            m = m_ref[row, :]
            m_new = jnp.maximum(m, jnp.max(s[i], axis=0, keepdims=True) + shift)
            m_ref[row, :] = m_new
            al_ref[slot, row, :] = jnp.exp2(m - m_new)
            p_ref[slot, i] = jnp.exp2(s[i] - (m_new - shift)).astype(BF16)

    def s2_matmul(j, slot):
        kb = key_block(j)
        return [_dot(vt_ref[kb, h * va:(h + 1) * va, :], p_ref[slot, i])
                for i, (h, c) in enumerate(chains)]

    def s2_finish(pv, slot):
        for i, (h, c) in enumerate(chains):
            acc_ref[h, c] = al_ref[slot, i:i + 1, :] * acc_ref[h, c] + pv[i]

    def step(j, slot):
        s = s1_matmul(j, False)
        pv = s2_matmul(j - 1, 1 - slot)
        s2_finish(pv, 1 - slot)
        s1_finish(s, j, slot)

    acc_ref[...] = jnp.zeros_like(acc_ref)
    m_ref[...] = jnp.full_like(m_ref, NEG_BIG)
    s1_finish(s1_matmul(0, True), 0, 0)

    def body(k, carry):
        step(2 * k + 1, 1)
        step(2 * k + 2, 0)
        return carry

    lax.fori_loop(0, qi // 2, body, 0)

    @pl.when((qi & 1) == 1)
    def _():
        step(qi, 1)
        s2_finish(s2_matmul(qi, 1), 1)

    @pl.when((qi & 1) == 0)
    def _():
        s2_finish(s2_matmul(qi, 0), 0)

    lp = lamp_ref[...]
    lam = (jnp.exp(jnp.sum(lp[0:1] * lp[1:2], axis=-1, keepdims=True))
           - jnp.exp(jnp.sum(lp[2:3] * lp[3:4], axis=-1, keepdims=True)) + LAMBDA_INIT)
    for h in range(DF_HEADS):
        a1, a2 = acc_ref[h, 0], acc_ref[h, 1]
        l1, l2 = a1[DF_V_DIM:DF_V_DIM + 1, :], a2[DF_V_DIM:DF_V_DIM + 1, :]
        out_t = a1[0:DF_V_DIM, :] / l1 - lam * (a2[0:DF_V_DIM, :] / l2)
        ms = jnp.mean(out_t * out_t, axis=0, keepdims=True)
        out = (out_t * lax.rsqrt(ms + SUBLN_EPS)).T * (sg_ref[...] * (1.0 - LAMBDA_INIT))
        o_ref[0, :, h * DF_V_DIM:(h + 1) * DF_V_DIM] = out.astype(o_ref.dtype)


def _df_attn(proj3d, lam_params, subln_g, wg, wu, wd):
    b, s, _ = proj3d.shape
    t = DF_T
    nq = s // t
    grid = (b, nq)
    steps = b * nq
    cq, ck, cv = OFF_DFQ // DF_QK_WIDTH, OFF_DFK // DF_QK_WIDTH, OFF_DFV // DF_V_WIDTH
    single = pl.Buffered(1)
    d, d_ff = wg.shape
    wg3 = wg.reshape(steps, d // steps, d_ff)
    wu3 = wu.reshape(steps, d // steps, d_ff)
    wd3 = wd.reshape(steps // 2, 2 * d_ff // steps, d)
    gu_spec = pl.BlockSpec((1,) + wg3.shape[1:], lambda bi, qi: (bi * nq + qi, 0, 0))
    dn_spec = pl.BlockSpec((1,) + wd3.shape[1:], lambda bi, qi: ((bi * nq + qi) // 2, 0, 0))
    out_b, wg_bf, wu_bf, wd_bf = pl.pallas_call(
        _df_attn_kernel,
        out_shape=(jax.ShapeDtypeStruct((b, s, DF_V_WIDTH), BF16),
                   jax.ShapeDtypeStruct(wg3.shape, BF16),
                   jax.ShapeDtypeStruct(wu3.shape, BF16),
                   jax.ShapeDtypeStruct(wd3.shape, BF16)),
        grid=grid,
        in_specs=[
            pl.BlockSpec((4, HEAD_DIM), lambda bi, qi: (0, 0)),
            pl.BlockSpec((1, t, DF_QK_WIDTH), lambda bi, qi: (bi, qi, cq)),
            pl.BlockSpec((1, s, DF_QK_WIDTH), lambda bi, qi: (bi, 0, ck), pipeline_mode=single),
            pl.BlockSpec((1, s, DF_V_WIDTH), lambda bi, qi: (bi, 0, cv), pipeline_mode=single),
            pl.BlockSpec((1, DF_V_DIM), lambda bi, qi: (0, 0)),
            gu_spec, gu_spec, dn_spec,
        ],
        out_specs=(pl.BlockSpec((1, t, DF_V_WIDTH), lambda bi, qi: (bi, qi, 0)),
                   gu_spec, gu_spec, dn_spec),
        scratch_shapes=[pltpu.VMEM((DF_HEADS, t, t), F32),
                        pltpu.VMEM((DF_HEADS, t, HEAD_DIM), BF16),
                        pltpu.VMEM((DF_HEADS, t, HEAD_DIM), BF16),
                        pltpu.VMEM((DF_HEADS, 2, DF_V_AUG, t), F32),
                        pltpu.VMEM((s // t, DF_HEADS * DF_V_AUG, t), BF16),
                        pltpu.VMEM((2, 2 * DF_HEADS, t, t), BF16),
                        pltpu.VMEM((2, 2 * DF_HEADS, t), F32),
                        pltpu.VMEM((2 * DF_HEADS, t), F32)],
        compiler_params=pltpu.CompilerParams(
            dimension_semantics=("arbitrary", "arbitrary"),
            vmem_limit_bytes=VMEM_LIMIT),
        name="df_attn",
    )(lam_params, proj3d, proj3d, proj3d, subln_g, wg3, wu3, wd3)
    return out_b, wg_bf.reshape(wg.shape), wu_bf.reshape(wu.shape), wd_bf.reshape(wd.shape)


def _merge_kernel(a_ref, wa_ref, b_ref, wb_ref, ga_ref, gb_ref, o_ref):
    pa = _dot(a_ref[...], wa_ref[...])
    pb = _dot(b_ref[...], wb_ref[...])
    o_ref[...] = (ga_ref[...].astype(F32) * pa + gb_ref[...].astype(F32) * pb).astype(o_ref.dtype)


def _merge(out_a, wa, out_b, wb, proj2d, tm=1024, tn=1024):
    t = out_a.shape[0]
    grid = (t // tm, D_MODEL // tn)
    ca, cb = OFF_GA // tn, OFF_GB // tn
    return pl.pallas_call(
        _merge_kernel,
        out_shape=jax.ShapeDtypeStruct((t, D_MODEL), BF16),
        grid=grid,
        in_specs=[
            pl.BlockSpec((tm, SB_WIDTH), lambda i, j: (i, 0)),
            pl.BlockSpec((SB_WIDTH, tn), lambda i, j: (0, j)),
            pl.BlockSpec((tm, DF_V_WIDTH), lambda i, j: (i, 0)),
            pl.BlockSpec((DF_V_WIDTH, tn), lambda i, j: (0, j)),
            pl.BlockSpec((tm, tn), lambda i, j: (i, ca + j)),
            pl.BlockSpec((tm, tn), lambda i, j: (i, cb + j)),
        ],
        out_specs=pl.BlockSpec((tm, tn), lambda i, j: (i, j)),
        compiler_params=pltpu.CompilerParams(
            dimension_semantics=("arbitrary", "arbitrary"),
            vmem_limit_bytes=VMEM_LIMIT),
        name="merge",
    )(out_a, wa, out_b, wb, proj2d, proj2d)


def _out_proj_kernel(m_ref, w_ref, x_ref, g_ref, h_ref, hn_ref):
    h = x_ref[...] + _dot(m_ref[...], w_ref[...])
    h_ref[...] = h
    ms = jnp.mean(h * h, axis=-1, keepdims=True)
    hn_ref[...] = (h * lax.rsqrt(ms + EPS) * g_ref[...]).astype(hn_ref.dtype)


def _out_proj(merged, w_out, x2d, g2, tm=512):
    t = merged.shape[0]
    grid = (t // tm,)
    return pl.pallas_call(
        _out_proj_kernel,
        out_shape=(jax.ShapeDtypeStruct((t, D_MODEL), F32),
                   jax.ShapeDtypeStruct((t, D_MODEL), BF16)),
        grid=grid,
        in_specs=[
            pl.BlockSpec((tm, D_MODEL), lambda i: (i, 0)),
            pl.BlockSpec((D_MODEL, D_MODEL), lambda i: (0, 0)),
            pl.BlockSpec((tm, D_MODEL), lambda i: (i, 0)),
            pl.BlockSpec((1, D_MODEL), lambda i: (0, 0)),
        ],
        out_specs=(pl.BlockSpec((tm, D_MODEL), lambda i: (i, 0)),
                   pl.BlockSpec((tm, D_MODEL), lambda i: (i, 0))),
        compiler_params=pltpu.CompilerParams(
            dimension_semantics=("arbitrary",),
            vmem_limit_bytes=VMEM_LIMIT),
        name="out_proj",
    )(merged, w_out, x2d, g2)


FFN_SUB = 2


def _ffn_kernel(hn_ref, h_ref, wg_ref, wu_ref, wd_ref, o_ref):
    j = pl.program_id(1)

    @pl.when(j == 0)
    def _():
        o_ref[...] = h_ref[...]

    hn = hn_ref[...]
    tf = wg_ref.shape[1]
    sub = tf // FFN_SUB
    gu = []
    for c in range(FFN_SUB):
        cols = slice(c * sub, (c + 1) * sub)
        gu.append((_dot(hn, wg_ref[:, cols]), _dot(hn, wu_ref[:, cols])))
    acc = None
    for c in range(FFN_SUB):
        g, u = gu[c]
        hid = (g * jax.nn.sigmoid(g) * u).astype(BF16)
        d = _dot(hid, wd_ref[c * sub:(c + 1) * sub, :])
        acc = d if acc is None else acc + d
    o_ref[...] += acc


def _ffn(hn, h, wg, wu, wd, tm=512, tf=512):
    t = hn.shape[0]
    d_ff = wg.shape[1]
    grid = (t // tm, d_ff // tf)
    return pl.pallas_call(
        _ffn_kernel,
        out_shape=jax.ShapeDtypeStruct((t, D_MODEL), F32),
        grid=grid,
        in_specs=[
            pl.BlockSpec((tm, D_MODEL), lambda i, j: (i, 0)),
            pl.BlockSpec((tm, D_MODEL), lambda i, j: (i, 0)),
            pl.BlockSpec((D_MODEL, tf), lambda i, j: (0, j)),
            pl.BlockSpec((D_MODEL, tf), lambda i, j: (0, j)),
            pl.BlockSpec((tf, D_MODEL), lambda i, j: (j, 0)),
        ],
        out_specs=pl.BlockSpec((tm, D_MODEL), lambda i, j: (i, 0)),
        compiler_params=pltpu.CompilerParams(
            dimension_semantics=("arbitrary", "arbitrary"),
            vmem_limit_bytes=VMEM_LIMIT),
        name="ffn",
    )(hn, h, wg, wu, wd)


def kernel(x, norm1_g, w_in, q_norm_g, k_norm_g, lambda_q1, lambda_k1, lambda_q2, lambda_k2,
           subln_g, w_branch_a, w_branch_b, w_out, norm2_g, w_ffn_gate, w_ffn_up, w_ffn_down):
    b, s, d = x.shape
    t = b * s
    layer = 0
    x2d = x.reshape(t, d)
    lam_params = jnp.stack([lambda_q1[layer], lambda_k1[layer], lambda_q2[layer], lambda_k2[layer]])

    proj = _in_proj(x2d, norm1_g[layer].reshape(1, d), w_in[layer].astype(BF16),
                    q_norm_g[layer].reshape(1, HEAD_DIM), k_norm_g[layer].reshape(1, HEAD_DIM))
    proj3d = proj.reshape(b, s, IN_WIDTH)

    out_a = _sb_attn(proj3d).reshape(t, SB_WIDTH)
    out_b, wg, wu, wd = _df_attn(proj3d, lam_params, subln_g[layer].reshape(1, DF_V_DIM),
                                 w_ffn_gate[layer], w_ffn_up[layer], w_ffn_down[layer])
    out_b = out_b.reshape(t, DF_V_WIDTH)

    merged = _merge(out_a, w_branch_a[layer].astype(BF16), out_b, w_branch_b[layer].astype(BF16), proj)
    h, hn = _out_proj(merged, w_out[layer].astype(BF16), x2d, norm2_g[layer].reshape(1, d))
    out = _ffn(hn, h, wg, wu, wd)
    return out.reshape(b, s, d)
```

```python
import math

import numpy as np
import jax
import jax.numpy as jnp
from jax import lax
from jax.experimental import pallas as pl
from jax.experimental.pallas import tpu as pltpu

F32 = jnp.float32
BF16 = jnp.bfloat16

D_MODEL = 2048
SB_HEADS = 8
HEAD_DIM = 128
DF_HEADS = 4
DF_V_DIM = 256
CHUNK = 64
SB_WIDTH = SB_HEADS * HEAD_DIM
DF_QK_WIDTH = DF_HEADS * 2 * HEAD_DIM
DF_V_WIDTH = DF_HEADS * DF_V_DIM
IN_WIDTH = 3 * SB_WIDTH + 2 * DF_QK_WIDTH + DF_V_WIDTH + 2 * D_MODEL
EPS = 1e-6
SUBLN_EPS = 1e-5
LAMBDA_INIT = 0.8 - 0.6 * math.exp(-0.3 * 0)
LOG2E = math.log2(math.e)
Q_SCALE = LOG2E / math.sqrt(HEAD_DIM)

OFF_SBQ = 0
OFF_SBK = SB_WIDTH
OFF_SBV = 2 * SB_WIDTH
OFF_DFQ = 3 * SB_WIDTH
OFF_DFK = OFF_DFQ + DF_QK_WIDTH
OFF_DFV = OFF_DFK + DF_QK_WIDTH
OFF_GA = OFF_DFV + DF_V_WIDTH
OFF_GB = OFF_GA + D_MODEL

VMEM_LIMIT = 56 * 1024 * 1024
NEG_BIG = -1e30


def _dot(a, b):
    return jnp.dot(a, b, preferred_element_type=F32)


def _dot_nt(a, b):
    return lax.dot_general(a, b, (((1,), (1,)), ((), ())), preferred_element_type=F32)


PROJ_TN = 1024


def _in_proj_kernel(x_ref, g_ref, w_ref, qg_ref, kg_ref, o_ref, xn_ref):
    j = pl.program_id(1)

    @pl.when(j == 0)
    def _():
        x = x_ref[...]
        ms = jnp.mean(x * x, axis=-1, keepdims=True)
        xn_ref[...] = (x * lax.rsqrt(ms + EPS) * g_ref[...]).astype(BF16)

    is_sbq = j == OFF_SBQ // PROJ_TN
    is_q = j == OFF_DFQ // PROJ_TN
    is_qk = is_q | (j == OFF_DFK // PROJ_TN)

    def proj():
        return _dot(xn_ref[...], w_ref[...])

    @pl.when(jnp.logical_not(is_qk))
    def _():
        o_ref[...] = (proj() * jnp.where(is_sbq, Q_SCALE, 1.0)).astype(o_ref.dtype)

    @pl.when(is_qk)
    def _():
        acc = proj()
        gain = jnp.where(is_q, qg_ref[...] * Q_SCALE, kg_ref[...])
        for c in range(PROJ_TN // HEAD_DIM):
            a = acc[:, c * HEAD_DIM:(c + 1) * HEAD_DIM]
            ms = jnp.mean(a * a, axis=-1, keepdims=True)
            o_ref[:, c * HEAD_DIM:(c + 1) * HEAD_DIM] = (
                a * lax.rsqrt(ms + EPS) * gain).astype(o_ref.dtype)


def _in_proj(x2d, g1, w_qkv, qg, kg, tm=1024):
    t = x2d.shape[0]
    grid = (t // tm, OFF_GA // PROJ_TN)
    return pl.pallas_call(
        _in_proj_kernel,
        out_shape=(jax.ShapeDtypeStruct((t, OFF_GA), BF16),
                   jax.ShapeDtypeStruct((t, D_MODEL), BF16)),
        grid=grid,
        in_specs=[
            pl.BlockSpec((tm, D_MODEL), lambda i, j: (i, 0)),
            pl.BlockSpec((1, D_MODEL), lambda i, j: (0, 0)),
            pl.BlockSpec((D_MODEL, PROJ_TN), lambda i, j: (0, j)),
            pl.BlockSpec((1, HEAD_DIM), lambda i, j: (0, 0)),
            pl.BlockSpec((1, HEAD_DIM), lambda i, j: (0, 0)),
        ],
        out_specs=(pl.BlockSpec((tm, PROJ_TN), lambda i, j: (i, j)),
                   pl.BlockSpec((tm, D_MODEL), lambda i, j: (i, 0))),
        compiler_params=pltpu.CompilerParams(
            dimension_semantics=("arbitrary", "arbitrary"),
            vmem_limit_bytes=VMEM_LIMIT),
        name="in_proj",
    )(x2d, g1, w_qkv, qg, kg)


SB_T = 256
SB_G = 4


def _softplus2(a):
    return jnp.maximum(a, 0.0) + jnp.log2(1.0 + jnp.exp2(jnp.minimum(a, -a)))


def _split_hi_lo(a):
    hi = a.astype(BF16)
    lo = (a - hi.astype(F32)).astype(BF16)
    return hi, lo


def _eye(n):
    return (lax.broadcasted_iota(jnp.int32, (n, n), 0)
            == lax.broadcasted_iota(jnp.int32, (n, n), 1)).astype(BF16)


def _fill_vt(v_ref, vt_ref, t, head_dim=HEAD_DIM, head_rows=HEAD_DIM):
    eye = _eye(HEAD_DIM)
    n_heads = v_ref.shape[2] // head_dim

    def body(c, _):
        start = pl.multiple_of(c * t, t)
        for h in range(n_heads):
            for g in range(head_dim // HEAD_DIM):
                lanes = slice(h * head_dim + g * HEAD_DIM, h * head_dim + (g + 1) * HEAD_DIM)
                rows = slice(h * head_rows + g * HEAD_DIM, h * head_rows + (g + 1) * HEAD_DIM)
                vt_ref[c, rows, :] = _dot_nt(eye, v_ref[0, pl.ds(start, t), lanes]).astype(BF16)
            if head_rows > head_dim:
                vt_ref[c, h * head_rows + head_dim:(h + 1) * head_rows, :] = jnp.ones(
                    (head_rows - head_dim, t), BF16)
        return 0

    lax.fori_loop(0, v_ref.shape[1] // t, body, 0)


SB_EXIT_LOG2 = 160.0
NORM_SLACK = 1.01


def _max_key_norm2(k_ref, kn_ref, t):
    ones8 = jnp.ones((8, HEAD_DIM), BF16)
    n_groups = k_ref.shape[2] // HEAD_DIM

    def body(c, mx):
        start = pl.multiple_of(c * t, t)
        out = []
        for g in range(n_groups):
            kf = k_ref[0, pl.ds(start, t), g * HEAD_DIM:(g + 1) * HEAD_DIM].astype(F32)
            out.append(jnp.maximum(mx[g], _dot_nt(ones8, (kf * kf).astype(BF16))))
        return tuple(out)

    mx = lax.fori_loop(0, k_ref.shape[1] // t, body,
                       tuple(jnp.zeros((8, t), F32) for _ in range(n_groups)))
    for g in range(n_groups):
        kn_ref[g:g + 1, :] = jnp.broadcast_to(
            jnp.max(mx[g][0:1, :], axis=1, keepdims=True), (1, HEAD_DIM))


GATE_CHUNKS = 8
GATE_AFTER_QK = 6
GATE_AFTER_CUM = 2


def _sb_attn_kernel(q_ref, k_ref, v_ref, xn_ref, wgate_ref, o_ref, gate_ref,
                    vt_ref, a_ref, hl_ref, e_ref, kn_ref, gbuf_ref):
    qi = pl.program_id(2)
    t = SB_T
    heads = range(SB_G)

    @pl.when(qi == 0)
    def _():
        _fill_vt(v_ref, vt_ref, t)
        _max_key_norm2(k_ref, kn_ref, t)

    r2 = lax.broadcasted_iota(jnp.int32, (t, 2 * t), 0)
    c2 = lax.broadcasted_iota(jnp.int32, (t, 2 * t), 1)
    l2 = ((c2 & (t - 1)) >= r2).astype(BF16)
    lanes = [slice(g * HEAD_DIM, (g + 1) * HEAD_DIM) for g in heads]

    def block_start(j):
        return pl.multiple_of(jnp.maximum(qi - j, 0) * t, t)

    def strict_mask():
        key = lax.broadcasted_iota(jnp.int32, (t, t), 0)
        qry = lax.broadcasted_iota(jnp.int32, (t, t), 1)
        return key < qry

    def s1_matmul(j):
        start = block_start(j)
        return [_dot_nt(k_ref[0, pl.ds(start, t), lanes[g]], q_ref[0, :, lanes[g]]) for g in heads]

    def s1_finish(a, slot, masked):
        for g in heads:
            sp = _softplus2(a[g])
            if masked:
                sp = jnp.where(strict_mask(), sp, 0.0)
            hi, lo = _split_hi_lo(sp)
            a_ref[slot, g] = a[g]
            hl_ref[slot, g, 0:t, :] = hi
            hl_ref[slot, g, t:2 * t, :] = lo

    def s2_matmul(slot):
        return [_dot(l2, hl_ref[slot, g]) for g in heads]

    def s2_finish(cum, slot, masked):
        for g in heads:
            e0 = a_ref[slot, g] - cum[g]
            if masked:
                e0 = jnp.where(strict_mask(), e0, NEG_BIG)
            e_ref[slot, g] = e0
        return tuple(c[0:1, :] for c in cum)

    def s3_weights(slot, carry):
        return [jnp.exp2(e_ref[slot, g] - carry[g]).astype(BF16) for g in heads]

    def s3_matmul(j, w):
        kb = jnp.maximum(qi - j, 0)
        return [_dot(vt_ref[kb, lanes[g], :], w[g]) for g in heads]

    def step(i, slot, st):
        acc, carry, sums = st
        w = s3_weights(slot, carry)
        a = s1_matmul(i)
        cum = s2_matmul(1 - slot)
        pv = s3_matmul(i - 2, w)
        s1_finish(a, slot, False)
        new_sums = s2_finish(cum, 1 - slot, False)
        return (tuple(acc[g] + pv[g] for g in heads),
                tuple(carry[g] + sums[g] for g in heads), new_sums)

    ones8 = jnp.ones((8, HEAD_DIM), BF16)
    bound = []
    for g in heads:
        qf = q_ref[0, :, lanes[g]].astype(F32)
        qn2 = _dot_nt(ones8, (qf * qf).astype(BF16))[0:1, :]
        bound.append(jnp.sqrt(qn2 * kn_ref[g:g + 1, 0:1]) * NORM_SLACK)

    def all_underflow(carry):
        slack = bound[0] - carry[0]
        for g in heads[1:]:
            slack = jnp.maximum(slack, bound[g] - carry[g])
        return jnp.max(slack) < -SB_EXIT_LOG2

    def gate_chunk(c):
        cols = slice(c * gate_cols, (c + 1) * gate_cols)
        gbuf_ref[:, cols] = _dot(xn_ref[0], wgate_ref[:, cols])

    gate_cols = wgate_ref.shape[1] // GATE_CHUNKS

    a0 = s1_matmul(0)
    a1 = s1_matmul(1)
    for c in range(0, GATE_AFTER_QK):
        gate_chunk(c)
    s1_finish(a0, 0, True)
    s1_finish(a1, 1, False)
    cum0 = s2_matmul(0)
    cum1 = s2_matmul(1)
    for c in range(GATE_AFTER_QK, GATE_AFTER_QK + GATE_AFTER_CUM):
        gate_chunk(c)
    sums0 = s2_finish(cum0, 0, True)
    sums1 = s2_finish(cum1, 1, False)
    w0 = s3_weights(0, tuple(jnp.zeros((1, t), F32) for _ in heads))
    w1 = s3_weights(1, sums0)
    pv0 = s3_matmul(0, w0)
    pv1 = s3_matmul(1, w1)
    for c in range(GATE_AFTER_QK + GATE_AFTER_CUM, GATE_CHUNKS):
        gate_chunk(c)
    gate_ref[0] = gbuf_ref[...].astype(gate_ref.dtype)
    acc2 = tuple(pv0[g] + jnp.where(qi > 0, pv1[g], 0.0) for g in heads)
    carry2 = tuple(sums0[g] + sums1[g] for g in heads)

    def remaining_blocks(acc, carry):
        s1_finish(s1_matmul(2), 0, False)
        a = s1_matmul(3)
        cum = s2_matmul(0)
        s1_finish(a, 1, False)
        sums = s2_finish(cum, 0, False)
        pairs = (qi - 2) // 2

        def cond(st):
            k, done = st[0], st[1]
            return (k < pairs) & jnp.logical_not(done)

        def body(st):
            k = st[0]
            acc, carry, sums = step(2 * k + 5, 1, step(2 * k + 4, 0, st[2:]))
            return (k + 1, all_underflow(carry), acc, carry, sums)

        k, _, acc, carry, sums = lax.while_loop(
            cond, body, (jnp.int32(0), False, acc, carry, sums))

        w = s3_weights(0, carry)
        cum = s2_matmul(1)
        pv = s3_matmul(2 * k + 2, w)
        s2_finish(cum, 1, False)
        acc = tuple(acc[g] + pv[g] for g in heads)
        carry = tuple(carry[g] + sums[g] for g in heads)
        pv = s3_matmul(2 * k + 3, s3_weights(1, carry))
        last_is_real = 2 * k + 3 <= qi
        return tuple(acc[g] + jnp.where(last_is_real, pv[g], 0.0) for g in heads)

    more = (qi >= 2) & jnp.logical_not(all_underflow(carry2))
    acc = lax.cond(more, lambda: remaining_blocks(acc2, carry2), lambda: acc2)
    for g in heads:
        o_ref[0, :, lanes[g]] = acc[g].T.astype(o_ref.dtype)


def _sb_attn(proj3d, xn3d, w_gate):
    b, s, _ = proj3d.shape
    t = SB_T
    w = SB_G * HEAD_DIM
    grid = (b, SB_HEADS // SB_G, s // t)
    cq, ck, cv = OFF_SBQ // w, OFF_SBK // w, OFF_SBV // w
    gate_w = w_gate.shape[1] // (SB_HEADS // SB_G)
    single = pl.Buffered(1)
    return pl.pallas_call(
        _sb_attn_kernel,
        out_shape=(jax.ShapeDtypeStruct((b, s, SB_WIDTH), BF16),
                   jax.ShapeDtypeStruct((b, s, w_gate.shape[1]), BF16)),
        grid=grid,
        in_specs=[
            pl.BlockSpec((1, t, w), lambda bi, h, qi: (bi, qi, cq + h)),
            pl.BlockSpec((1, s, w), lambda bi, h, qi: (bi, 0, ck + h), pipeline_mode=single),
            pl.BlockSpec((1, s, w), lambda bi, h, qi: (bi, 0, cv + h), pipeline_mode=single),
            pl.BlockSpec((1, t, D_MODEL), lambda bi, h, qi: (bi, qi, 0)),
            pl.BlockSpec((D_MODEL, gate_w), lambda bi, h, qi: (0, h), pipeline_mode=single),
        ],
        out_specs=(pl.BlockSpec((1, t, w), lambda bi, h, qi: (bi, qi, h)),
                   pl.BlockSpec((1, t, gate_w), lambda bi, h, qi: (bi, qi, h))),
        scratch_shapes=[pltpu.VMEM((s // t, w, t), BF16),
                        pltpu.VMEM((2, SB_G, t, t), F32),
                        pltpu.VMEM((2, SB_G, 2 * t, t), BF16),
                        pltpu.VMEM((2, SB_G, t, t), F32),
                        pltpu.VMEM((SB_G, HEAD_DIM), F32),
                        pltpu.VMEM((t, gate_w), F32)],
        compiler_params=pltpu.CompilerParams(
            dimension_semantics=("arbitrary", "arbitrary", "arbitrary"),
            vmem_limit_bytes=VMEM_LIMIT),
        name="sb_attn",
    )(proj3d, proj3d, proj3d, xn3d, w_gate)


DF_T = 256
DF_SLOPES = [2.0 ** (-8.0 * (h + 1) / DF_HEADS) for h in range(DF_HEADS)]
DF_V_AUG = DF_V_DIM + 16
DF_SLOPE_PIECES = 4


def _bf16_pieces(x, n):
    pieces = []
    for _ in range(n):
        p = float(np.asarray(x, dtype=np.float32).astype(jnp.bfloat16).astype(np.float32))
        pieces.append(p)
        x = x - p
    return pieces


def _df_attn_kernel(lamp_ref, q_ref, k_ref, v_ref, sg_ref, wg_ref, wu_ref, wd_ref,
                    o_ref, wg_out_ref, wu_out_ref, wd_out_ref,
                    bias_ref, kc_ref, qc_ref, acc_ref, vt_ref, p_ref, al_ref, m_ref):
    bi = pl.program_id(0)
    qi = pl.program_id(1)
    t = DF_T
    va = DF_V_AUG

    wg_out_ref[...] = wg_ref[...].astype(BF16)
    wu_out_ref[...] = wu_ref[...].astype(BF16)

    @pl.when((qi & 1) == 0)
    def _():
        wd_out_ref[...] = wd_ref[...].astype(BF16)

    @pl.when(qi == 0)
    def _():
        _fill_vt(v_ref, vt_ref, t, DF_V_DIM, va)

    @pl.when((bi == 0) & (qi == 0))
    def _():
        key = lax.broadcasted_iota(jnp.int32, (t, t), 0)
        qry = lax.broadcasted_iota(jnp.int32, (t, t), 1)
        rel = (qry - key).astype(F32)
        allowed = (key // CHUNK) <= (qry // CHUNK)
        pos = lax.broadcasted_iota(jnp.int32, (t, HEAD_DIM), 0).astype(F32)
        col = lax.broadcasted_iota(jnp.int32, (t, HEAD_DIM), 1)
        n = DF_SLOPE_PIECES
        for h in range(DF_HEADS):
            sl = DF_SLOPES[h] * LOG2E
            bias_ref[h] = jnp.where(allowed, -sl * jnp.abs(rel), NEG_BIG)
            pieces = jnp.zeros((t, HEAD_DIM), F32)
            for idx, piece in enumerate(_bf16_pieces(sl, n)):
                pieces = jnp.where((col == idx) | (col == n + idx), piece, pieces)
            kc_ref[h] = jnp.where(col < n, pos, jnp.where(col < 2 * n, pieces, 0.0)).astype(BF16)
            qc_ref[h] = jnp.where(col < n, pieces, jnp.where(col < 2 * n, -pos, 0.0)).astype(BF16)

    chains = [(h, c) for h in range(DF_HEADS) for c in range(2)]

    def key_block(j):
        return jnp.where(j == 0, qi, j - 1)

    def s1_matmul(j, diagonal):
        start = pl.multiple_of(key_block(j) * t, t)
        out = []
        for h, c in chains:
            lanes = slice((2 * h + c) * HEAD_DIM, (2 * h + c + 1) * HEAD_DIM)
            kb, qb = k_ref[0, pl.ds(start, t), lanes], q_ref[0, :, lanes]
            if diagonal:
                out.append(_dot_nt(kb, qb) + bias_ref[h])
            else:
                out.append(_dot_nt(jnp.concatenate([kb, kc_ref[h]], axis=1),
                                   jnp.concatenate([qb, qc_ref[h]], axis=1)))
        return out

    def s1_finish(s, j, slot):
        dist = ((qi - key_block(j)) * t).astype(F32)
        for i, (h, c) in enumerate(chains):
            row = slice(i, i + 1)
            shift = -(DF_SLOPES[h] * LOG2E) * dist
            m = m_ref[row, :]
            m_new = jnp.maximum(m, jnp.max(s[i], axis=0, keepdims=True) + shift)
            m_ref[row, :] = m_new
            al_ref[slot, row, :] = jnp.exp2(m - m_new)
            p_ref[slot, i] = jnp.exp2(s[i] - (m_new - shift)).astype(BF16)

    def s2_matmul(j, slot):
        kb = key_block(j)
        return [_dot(vt_ref[kb, h * va:(h + 1) * va, :], p_ref[slot, i])
                for i, (h, c) in enumerate(chains)]

    def s2_finish(pv, slot):
        for i, (h, c) in enumerate(chains):
            acc_ref[h, c] = al_ref[slot, i:i + 1, :] * acc_ref[h, c] + pv[i]

    def step(j, slot):
        s = s1_matmul(j, False)
        pv = s2_matmul(j - 1, 1 - slot)
        s2_finish(pv, 1 - slot)
        s1_finish(s, j, slot)

    acc_ref[...] = jnp.zeros_like(acc_ref)
    m_ref[...] = jnp.full_like(m_ref, NEG_BIG)
    s1_finish(s1_matmul(0, True), 0, 0)

    def body(k, carry):
        step(2 * k + 1, 1)
        step(2 * k + 2, 0)
        return carry

    lax.fori_loop(0, qi // 2, body, 0)

    @pl.when((qi & 1) == 1)
    def _():
        step(qi, 1)
        s2_finish(s2_matmul(qi, 1), 1)

    @pl.when((qi & 1) == 0)
    def _():
        s2_finish(s2_matmul(qi, 0), 0)

    lp = lamp_ref[...]
    lam = (jnp.exp(jnp.sum(lp[0:1] * lp[1:2], axis=-1, keepdims=True))
           - jnp.exp(jnp.sum(lp[2:3] * lp[3:4], axis=-1, keepdims=True)) + LAMBDA_INIT)
    for h in range(DF_HEADS):
        a1, a2 = acc_ref[h, 0], acc_ref[h, 1]
        l1, l2 = a1[DF_V_DIM:DF_V_DIM + 1, :], a2[DF_V_DIM:DF_V_DIM + 1, :]
        out_t = a1[0:DF_V_DIM, :] / l1 - lam * (a2[0:DF_V_DIM, :] / l2)
        ms = jnp.mean(out_t * out_t, axis=0, keepdims=True)
        out = (out_t * lax.rsqrt(ms + SUBLN_EPS)).T * (sg_ref[...] * (1.0 - LAMBDA_INIT))
        o_ref[0, :, h * DF_V_DIM:(h + 1) * DF_V_DIM] = out.astype(o_ref.dtype)


def _df_attn(proj3d, lam_params, subln_g, wg, wu, wd):
    b, s, _ = proj3d.shape
    t = DF_T
    nq = s // t
    grid = (b, nq)
    steps = b * nq
    cq, ck, cv = OFF_DFQ // DF_QK_WIDTH, OFF_DFK // DF_QK_WIDTH, OFF_DFV // DF_V_WIDTH
    single = pl.Buffered(1)
    d, d_ff = wg.shape
    wg3 = wg.reshape(steps, d // steps, d_ff)
    wu3 = wu.reshape(steps, d // steps, d_ff)
    wd3 = wd.reshape(steps // 2, 2 * d_ff // steps, d)
    gu_spec = pl.BlockSpec((1,) + wg3.shape[1:], lambda bi, qi: (bi * nq + qi, 0, 0))
    dn_spec = pl.BlockSpec((1,) + wd3.shape[1:], lambda bi, qi: ((bi * nq + qi) // 2, 0, 0))
    out_b, wg_bf, wu_bf, wd_bf = pl.pallas_call(
        _df_attn_kernel,
        out_shape=(jax.ShapeDtypeStruct((b, s, DF_V_WIDTH), BF16),
                   jax.ShapeDtypeStruct(wg3.shape, BF16),
                   jax.ShapeDtypeStruct(wu3.shape, BF16),
                   jax.ShapeDtypeStruct(wd3.shape, BF16)),
        grid=grid,
        in_specs=[
            pl.BlockSpec((4, HEAD_DIM), lambda bi, qi: (0, 0)),
            pl.BlockSpec((1, t, DF_QK_WIDTH), lambda bi, qi: (bi, qi, cq)),
            pl.BlockSpec((1, s, DF_QK_WIDTH), lambda bi, qi: (bi, 0, ck), pipeline_mode=single),
            pl.BlockSpec((1, s, DF_V_WIDTH), lambda bi, qi: (bi, 0, cv), pipeline_mode=single),
            pl.BlockSpec((1, DF_V_DIM), lambda bi, qi: (0, 0)),
            gu_spec, gu_spec, dn_spec,
        ],
        out_specs=(pl.BlockSpec((1, t, DF_V_WIDTH), lambda bi, qi: (bi, qi, 0)),
                   gu_spec, gu_spec, dn_spec),
        scratch_shapes=[pltpu.VMEM((DF_HEADS, t, t), F32),
                        pltpu.VMEM((DF_HEADS, t, HEAD_DIM), BF16),
                        pltpu.VMEM((DF_HEADS, t, HEAD_DIM), BF16),
                        pltpu.VMEM((DF_HEADS, 2, DF_V_AUG, t), F32),
                        pltpu.VMEM((s // t, DF_HEADS * DF_V_AUG, t), BF16),
                        pltpu.VMEM((2, 2 * DF_HEADS, t, t), BF16),
                        pltpu.VMEM((2, 2 * DF_HEADS, t), F32),
                        pltpu.VMEM((2 * DF_HEADS, t), F32)],
        compiler_params=pltpu.CompilerParams(
            dimension_semantics=("arbitrary", "arbitrary"),
            vmem_limit_bytes=VMEM_LIMIT),
        name="df_attn",
    )(lam_params, proj3d, proj3d, proj3d, subln_g, wg3, wu3, wd3)
    return out_b, wg_bf.reshape(wg.shape), wu_bf.reshape(wu.shape), wd_bf.reshape(wd.shape)


def _merge_kernel(a_ref, wa_ref, b_ref, wb_ref, ga_ref, gb_ref, o_ref):
    pa = _dot(a_ref[...], wa_ref[...])
    pb = _dot(b_ref[...], wb_ref[...])
    ga = jax.nn.sigmoid(ga_ref[...].astype(F32))
    gb = jax.nn.sigmoid(gb_ref[...].astype(F32))
    o_ref[...] = (ga * pa + gb * pb).astype(o_ref.dtype)


def _merge(out_a, wa, out_b, wb, gates, tm=1024, tn=1024):
    t = out_a.shape[0]
    grid = (t // tm, D_MODEL // tn)
    ca, cb = 0, D_MODEL // tn
    return pl.pallas_call(
        _merge_kernel,
        out_shape=jax.ShapeDtypeStruct((t, D_MODEL), BF16),
        grid=grid,
        in_specs=[
            pl.BlockSpec((tm, SB_WIDTH), lambda i, j: (i, 0)),
            pl.BlockSpec((SB_WIDTH, tn), lambda i, j: (0, j)),
            pl.BlockSpec((tm, DF_V_WIDTH), lambda i, j: (i, 0)),
            pl.BlockSpec((DF_V_WIDTH, tn), lambda i, j: (0, j)),
            pl.BlockSpec((tm, tn), lambda i, j: (i, ca + j)),
            pl.BlockSpec((tm, tn), lambda i, j: (i, cb + j)),
        ],
        out_specs=pl.BlockSpec((tm, tn), lambda i, j: (i, j)),
        compiler_params=pltpu.CompilerParams(
            dimension_semantics=("arbitrary", "arbitrary"),
            vmem_limit_bytes=VMEM_LIMIT),
        name="merge",
    )(out_a, wa, out_b, wb, gates, gates)


def _out_proj_kernel(m_ref, w_ref, x_ref, g_ref, h_ref, hn_ref):
    h = x_ref[...] + _dot(m_ref[...], w_ref[...])
    h_ref[...] = h
    ms = jnp.mean(h * h, axis=-1, keepdims=True)
    hn_ref[...] = (h * lax.rsqrt(ms + EPS) * g_ref[...]).astype(hn_ref.dtype)


def _out_proj(merged, w_out, x2d, g2, tm=512):
    t = merged.shape[0]
    grid = (t // tm,)
    return pl.pallas_call(
        _out_proj_kernel,
        out_shape=(jax.ShapeDtypeStruct((t, D_MODEL), F32),
                   jax.ShapeDtypeStruct((t, D_MODEL), BF16)),
        grid=grid,
        in_specs=[
            pl.BlockSpec((tm, D_MODEL), lambda i: (i, 0)),
            pl.BlockSpec((D_MODEL, D_MODEL), lambda i: (0, 0)),
            pl.BlockSpec((tm, D_MODEL), lambda i: (i, 0)),
            pl.BlockSpec((1, D_MODEL), lambda i: (0, 0)),
        ],
        out_specs=(pl.BlockSpec((tm, D_MODEL), lambda i: (i, 0)),
                   pl.BlockSpec((tm, D_MODEL), lambda i: (i, 0))),
        compiler_params=pltpu.CompilerParams(
            dimension_semantics=("arbitrary",),
            vmem_limit_bytes=VMEM_LIMIT),
        name="out_proj",
    )(merged, w_out, x2d, g2)


FFN_SUB = 2


def _ffn_kernel(hn_ref, h_ref, wg_ref, wu_ref, wd_ref, o_ref):
    j = pl.program_id(1)

    @pl.when(j == 0)
    def _():
        o_ref[...] = h_ref[...]

    hn = hn_ref[...]
    tf = wg_ref.shape[1]
    sub = tf // FFN_SUB
    gu = []
    for c in range(FFN_SUB):
        cols = slice(c * sub, (c + 1) * sub)
        gu.append((_dot(hn, wg_ref[:, cols]), _dot(hn, wu_ref[:, cols])))
    acc = None
    for c in range(FFN_SUB):
        g, u = gu[c]
        hid = (g * jax.nn.sigmoid(g) * u).astype(BF16)
        d = _dot(hid, wd_ref[c * sub:(c + 1) * sub, :])
        acc = d if acc is None else acc + d
    o_ref[...] += acc


def _ffn(hn, h, wg, wu, wd, tm=512, tf=512):
    t = hn.shape[0]
    d_ff = wg.shape[1]
    grid = (t // tm, d_ff // tf)
    return pl.pallas_call(
        _ffn_kernel,
        out_shape=jax.ShapeDtypeStruct((t, D_MODEL), F32),
        grid=grid,
        in_specs=[
            pl.BlockSpec((tm, D_MODEL), lambda i, j: (i, 0)),
            pl.BlockSpec((tm, D_MODEL), lambda i, j: (i, 0)),
            pl.BlockSpec((D_MODEL, tf), lambda i, j: (0, j)),
            pl.BlockSpec((D_MODEL, tf), lambda i, j: (0, j)),
            pl.BlockSpec((tf, D_MODEL), lambda i, j: (j, 0)),
        ],
        out_specs=pl.BlockSpec((tm, D_MODEL), lambda i, j: (i, 0)),
        compiler_params=pltpu.CompilerParams(
            dimension_semantics=("arbitrary", "arbitrary"),
            vmem_limit_bytes=VMEM_LIMIT),
        name="ffn",
    )(hn, h, wg, wu, wd)


def kernel(x, norm1_g, w_in, q_norm_g, k_norm_g, lambda_q1, lambda_k1, lambda_q2, lambda_k2,
           subln_g, w_branch_a, w_branch_b, w_out, norm2_g, w_ffn_gate, w_ffn_up, w_ffn_down):
    b, s, d = x.shape
    t = b * s
    layer = 0
    x2d = x.reshape(t, d)
    lam_params = jnp.stack([lambda_q1[layer], lambda_k1[layer], lambda_q2[layer], lambda_k2[layer]])

    w_in_bf = w_in[layer].astype(BF16)
    proj, xn = _in_proj(x2d, norm1_g[layer].reshape(1, d), w_in_bf[:, :OFF_GA],
                        q_norm_g[layer].reshape(1, HEAD_DIM), k_norm_g[layer].reshape(1, HEAD_DIM))
    proj3d = proj.reshape(b, s, OFF_GA)

    out_a, gates = _sb_attn(proj3d, xn.reshape(b, s, d), w_in_bf[:, OFF_GA:])
    out_a = out_a.reshape(t, SB_WIDTH)
    out_b, wg, wu, wd = _df_attn(proj3d, lam_params, subln_g[layer].reshape(1, DF_V_DIM),
                                 w_ffn_gate[layer], w_ffn_up[layer], w_ffn_down[layer])
    out_b = out_b.reshape(t, DF_V_WIDTH)

    merged = _merge(out_a, w_branch_a[layer].astype(BF16), out_b, w_branch_b[layer].astype(BF16),
                    gates.reshape(t, 2 * d))
    h, hn = _out_proj(merged, w_out[layer].astype(BF16), x2d, norm2_g[layer].reshape(1, d))
    out = _ffn(hn, h, wg, wu, wd)
    return out.reshape(b, s, d)
```

```python
import math

import numpy as np
import jax
import jax.numpy as jnp
from jax import lax
from jax.experimental import pallas as pl
from jax.experimental.pallas import tpu as pltpu

F32 = jnp.float32
BF16 = jnp.bfloat16

D_MODEL = 2048
SB_HEADS = 8
HEAD_DIM = 128
DF_HEADS = 4
DF_V_DIM = 256
CHUNK = 64
SB_WIDTH = SB_HEADS * HEAD_DIM
DF_QK_WIDTH = DF_HEADS * 2 * HEAD_DIM
DF_V_WIDTH = DF_HEADS * DF_V_DIM
IN_WIDTH = 3 * SB_WIDTH + 2 * DF_QK_WIDTH + DF_V_WIDTH + 2 * D_MODEL
EPS = 1e-6
SUBLN_EPS = 1e-5
LAMBDA_INIT = 0.8 - 0.6 * math.exp(-0.3 * 0)
LOG2E = math.log2(math.e)
Q_SCALE = LOG2E / math.sqrt(HEAD_DIM)

OFF_SBQ = 0
OFF_SBK = SB_WIDTH
OFF_SBV = 2 * SB_WIDTH
OFF_DFQ = 3 * SB_WIDTH
OFF_DFK = OFF_DFQ + DF_QK_WIDTH
OFF_DFV = OFF_DFK + DF_QK_WIDTH
OFF_GA = OFF_DFV + DF_V_WIDTH
OFF_GB = OFF_GA + D_MODEL

VMEM_LIMIT = 56 * 1024 * 1024
NEG_BIG = -1e30


def _dot(a, b):
    return jnp.dot(a, b, preferred_element_type=F32)


def _dot_nt(a, b):
    return lax.dot_general(a, b, (((1,), (1,)), ((), ())), preferred_element_type=F32)


PROJ_TN = 1024


def _in_proj_kernel(x_ref, g_ref, w_ref, qg_ref, kg_ref, o_ref, xn_ref):
    j = pl.program_id(1)

    @pl.when(j == 0)
    def _():
        x = x_ref[...]
        ms = jnp.mean(x * x, axis=-1, keepdims=True)
        xn_ref[...] = (x * lax.rsqrt(ms + EPS) * g_ref[...]).astype(BF16)

    is_sbq = j == OFF_SBQ // PROJ_TN
    is_q = j == OFF_DFQ // PROJ_TN
    is_qk = is_q | (j == OFF_DFK // PROJ_TN)

    def proj():
        return _dot(xn_ref[...], w_ref[...])

    @pl.when(jnp.logical_not(is_qk))
    def _():
        o_ref[...] = (proj() * jnp.where(is_sbq, Q_SCALE, 1.0)).astype(o_ref.dtype)

    @pl.when(is_qk)
    def _():
        acc = proj()
        gain = jnp.where(is_q, qg_ref[...] * Q_SCALE, kg_ref[...])
        for c in range(PROJ_TN // HEAD_DIM):
            a = acc[:, c * HEAD_DIM:(c + 1) * HEAD_DIM]
            ms = jnp.mean(a * a, axis=-1, keepdims=True)
            o_ref[:, c * HEAD_DIM:(c + 1) * HEAD_DIM] = (
                a * lax.rsqrt(ms + EPS) * gain).astype(o_ref.dtype)


def _in_proj(x2d, g1, w_qkv, qg, kg, tm=1024):
    t = x2d.shape[0]
    grid = (t // tm, OFF_GA // PROJ_TN)
    return pl.pallas_call(
        _in_proj_kernel,
        out_shape=(jax.ShapeDtypeStruct((t, OFF_GA), BF16),
                   jax.ShapeDtypeStruct((t, D_MODEL), BF16)),
        grid=grid,
        in_specs=[
            pl.BlockSpec((tm, D_MODEL), lambda i, j: (i, 0)),
            pl.BlockSpec((1, D_MODEL), lambda i, j: (0, 0)),
            pl.BlockSpec((D_MODEL, PROJ_TN), lambda i, j: (0, j)),
            pl.BlockSpec((1, HEAD_DIM), lambda i, j: (0, 0)),
            pl.BlockSpec((1, HEAD_DIM), lambda i, j: (0, 0)),
        ],
        out_specs=(pl.BlockSpec((tm, PROJ_TN), lambda i, j: (i, j)),
                   pl.BlockSpec((tm, D_MODEL), lambda i, j: (i, 0))),
        compiler_params=pltpu.CompilerParams(
            dimension_semantics=("arbitrary", "arbitrary"),
            vmem_limit_bytes=VMEM_LIMIT),
        name="in_proj",
    )(x2d, g1, w_qkv, qg, kg)


SB_T = 256
SB_G = 4


def _softplus2(a):
    return jnp.maximum(a, 0.0) + jnp.log2(1.0 + jnp.exp2(jnp.minimum(a, -a)))


def _split_hi_lo(a):
    hi = a.astype(BF16)
    lo = (a - hi.astype(F32)).astype(BF16)
    return hi, lo


def _eye(n):
    return (lax.broadcasted_iota(jnp.int32, (n, n), 0)
            == lax.broadcasted_iota(jnp.int32, (n, n), 1)).astype(BF16)


def _fill_vt(v_ref, vt_ref, t, head_dim=HEAD_DIM, head_rows=HEAD_DIM):
    eye = _eye(HEAD_DIM)
    n_heads = v_ref.shape[2] // head_dim

    def body(c, _):
        start = pl.multiple_of(c * t, t)
        for h in range(n_heads):
            for g in range(head_dim // HEAD_DIM):
                lanes = slice(h * head_dim + g * HEAD_DIM, h * head_dim + (g + 1) * HEAD_DIM)
                rows = slice(h * head_rows + g * HEAD_DIM, h * head_rows + (g + 1) * HEAD_DIM)
                vt_ref[c, rows, :] = _dot_nt(eye, v_ref[0, pl.ds(start, t), lanes]).astype(BF16)
            if head_rows > head_dim:
                vt_ref[c, h * head_rows + head_dim:(h + 1) * head_rows, :] = jnp.ones(
                    (head_rows - head_dim, t), BF16)
        return 0

    lax.fori_loop(0, v_ref.shape[1] // t, body, 0)


SB_EXIT_LOG2 = 160.0
NORM_SLACK = 1.01


def _max_key_norm2(k_ref, kn_ref, t):
    ones8 = jnp.ones((8, HEAD_DIM), BF16)
    n_groups = k_ref.shape[2] // HEAD_DIM

    def body(c, mx):
        start = pl.multiple_of(c * t, t)
        out = []
        for g in range(n_groups):
            kf = k_ref[0, pl.ds(start, t), g * HEAD_DIM:(g + 1) * HEAD_DIM].astype(F32)
            out.append(jnp.maximum(mx[g], _dot_nt(ones8, (kf * kf).astype(BF16))))
        return tuple(out)

    mx = lax.fori_loop(0, k_ref.shape[1] // t, body,
                       tuple(jnp.zeros((8, t), F32) for _ in range(n_groups)))
    for g in range(n_groups):
        kn_ref[g:g + 1, :] = jnp.broadcast_to(
            jnp.max(mx[g][0:1, :], axis=1, keepdims=True), (1, HEAD_DIM))


GATE_CHUNKS = 8
GATE_AFTER_QK = 6
GATE_AFTER_CUM = 2


def _sb_attn_kernel(q_ref, k_ref, v_ref, xn_ref, wgate_ref, o_ref, gate_ref,
                    vt_ref, a_ref, hl_ref, e_ref, kn_ref, gbuf_ref):
    qi = pl.program_id(2)
    t = SB_T
    heads = range(SB_G)

    @pl.when(qi == 0)
    def _():
        _fill_vt(v_ref, vt_ref, t)
        _max_key_norm2(k_ref, kn_ref, t)

    r2 = lax.broadcasted_iota(jnp.int32, (t, 2 * t), 0)
    c2 = lax.broadcasted_iota(jnp.int32, (t, 2 * t), 1)
    l2 = ((c2 & (t - 1)) >= r2).astype(BF16)
    lanes = [slice(g * HEAD_DIM, (g + 1) * HEAD_DIM) for g in heads]

    def block_start(j):
        return pl.multiple_of(jnp.maximum(qi - j, 0) * t, t)

    def strict_mask():
        key = lax.broadcasted_iota(jnp.int32, (t, t), 0)
        qry = lax.broadcasted_iota(jnp.int32, (t, t), 1)
        return key < qry

    def s1_matmul(j):
        start = block_start(j)
        return [_dot_nt(k_ref[0, pl.ds(start, t), lanes[g]], q_ref[0, :, lanes[g]]) for g in heads]

    def s1_finish(a, slot, masked):
        for g in heads:
            sp = _softplus2(a[g])
            if masked:
                sp = jnp.where(strict_mask(), sp, 0.0)
            hi, lo = _split_hi_lo(sp)
            a_ref[slot, g] = a[g]
            hl_ref[slot, g, 0:t, :] = hi
            hl_ref[slot, g, t:2 * t, :] = lo

    def s2_matmul(slot):
        return [_dot(l2, hl_ref[slot, g]) for g in heads]

    def s2_finish(cum, slot, masked):
        for g in heads:
            e0 = a_ref[slot, g] - cum[g]
            if masked:
                e0 = jnp.where(strict_mask(), e0, NEG_BIG)
            e_ref[slot, g] = e0
        return tuple(c[0:1, :] for c in cum)

    def s3_weights(slot, carry):
        return [jnp.exp2(e_ref[slot, g] - carry[g]).astype(BF16) for g in heads]

    def s3_matmul(j, w):
        kb = jnp.maximum(qi - j, 0)
        return [_dot(vt_ref[kb, lanes[g], :], w[g]) for g in heads]

    def step(i, slot, st):
        acc, carry, sums = st
        w = s3_weights(slot, carry)
        a = s1_matmul(i)
        cum = s2_matmul(1 - slot)
        pv = s3_matmul(i - 2, w)
        s1_finish(a, slot, False)
        new_sums = s2_finish(cum, 1 - slot, False)
        return (tuple(acc[g] + pv[g] for g in heads),
                tuple(carry[g] + sums[g] for g in heads), new_sums)

    ones8 = jnp.ones((8, HEAD_DIM), BF16)
    bound = []
    for g in heads:
        qf = q_ref[0, :, lanes[g]].astype(F32)
        qn2 = _dot_nt(ones8, (qf * qf).astype(BF16))[0:1, :]
        bound.append(jnp.sqrt(qn2 * kn_ref[g:g + 1, 0:1]) * NORM_SLACK)

    def all_underflow(carry):
        slack = bound[0] - carry[0]
        for g in heads[1:]:
            slack = jnp.maximum(slack, bound[g] - carry[g])
        return jnp.max(slack) < -SB_EXIT_LOG2

    def gate_chunk(c):
        cols = slice(c * gate_cols, (c + 1) * gate_cols)
        gbuf_ref[:, cols] = _dot(xn_ref[0], wgate_ref[:, cols])

    gate_cols = wgate_ref.shape[1] // GATE_CHUNKS

    a0 = s1_matmul(0)
    a1 = s1_matmul(1)
    for c in range(0, GATE_AFTER_QK):
        gate_chunk(c)
    s1_finish(a0, 0, True)
    s1_finish(a1, 1, False)
    cum0 = s2_matmul(0)
    cum1 = s2_matmul(1)
    for c in range(GATE_AFTER_QK, GATE_AFTER_QK + GATE_AFTER_CUM):
        gate_chunk(c)
    sums0 = s2_finish(cum0, 0, True)
    sums1 = s2_finish(cum1, 1, False)
    w0 = s3_weights(0, tuple(jnp.zeros((1, t), F32) for _ in heads))
    w1 = s3_weights(1, sums0)
    pv0 = s3_matmul(0, w0)
    pv1 = s3_matmul(1, w1)
    for c in range(GATE_AFTER_QK + GATE_AFTER_CUM, GATE_CHUNKS):
        gate_chunk(c)
    gate_ref[0] = gbuf_ref[...].astype(gate_ref.dtype)
    acc2 = tuple(pv0[g] + jnp.where(qi > 0, pv1[g], 0.0) for g in heads)
    carry2 = tuple(sums0[g] + sums1[g] for g in heads)

    def remaining_blocks(acc, carry):
        s1_finish(s1_matmul(2), 0, False)
        a = s1_matmul(3)
        cum = s2_matmul(0)
        s1_finish(a, 1, False)
        sums = s2_finish(cum, 0, False)
        pairs = (qi - 2) // 2

        def cond(st):
            k, done = st[0], st[1]
            return (k < pairs) & jnp.logical_not(done)

        def body(st):
            k = st[0]
            acc, carry, sums = step(2 * k + 5, 1, step(2 * k + 4, 0, st[2:]))
            return (k + 1, all_underflow(carry), acc, carry, sums)

        k, _, acc, carry, sums = lax.while_loop(
            cond, body, (jnp.int32(0), False, acc, carry, sums))

        w = s3_weights(0, carry)
        cum = s2_matmul(1)
        pv = s3_matmul(2 * k + 2, w)
        s2_finish(cum, 1, False)
        acc = tuple(acc[g] + pv[g] for g in heads)
        carry = tuple(carry[g] + sums[g] for g in heads)
        pv = s3_matmul(2 * k + 3, s3_weights(1, carry))
        last_is_real = 2 * k + 3 <= qi
        return tuple(acc[g] + jnp.where(last_is_real, pv[g], 0.0) for g in heads)

    more = (qi >= 2) & jnp.logical_not(all_underflow(carry2))
    acc = lax.cond(more, lambda: remaining_blocks(acc2, carry2), lambda: acc2)
    for g in heads:
        o_ref[0, :, lanes[g]] = acc[g].T.astype(o_ref.dtype)


def _sb_attn(proj3d, xn3d, w_in_bf):
    b, s, _ = proj3d.shape
    t = SB_T
    w = SB_G * HEAD_DIM
    groups = SB_HEADS // SB_G
    grid = (b, groups, s // t)
    cq, ck, cv = OFF_SBQ // w, OFF_SBK // w, OFF_SBV // w
    gate_w = 2 * D_MODEL // groups
    cg = OFF_GA // gate_w
    single = pl.Buffered(1)
    return pl.pallas_call(
        _sb_attn_kernel,
        out_shape=(jax.ShapeDtypeStruct((b, s, SB_WIDTH), BF16),
                   jax.ShapeDtypeStruct((b, s, 2 * D_MODEL), BF16)),
        grid=grid,
        in_specs=[
            pl.BlockSpec((1, t, w), lambda bi, h, qi: (bi, qi, cq + h)),
            pl.BlockSpec((1, s, w), lambda bi, h, qi: (bi, 0, ck + h)),
            pl.BlockSpec((1, s, w), lambda bi, h, qi: (bi, 0, cv + h)),
            pl.BlockSpec((1, t, D_MODEL), lambda bi, h, qi: (bi, qi, 0)),
            pl.BlockSpec((D_MODEL, gate_w), lambda bi, h, qi: (0, cg + h), pipeline_mode=single),
        ],
        out_specs=(pl.BlockSpec((1, t, w), lambda bi, h, qi: (bi, qi, h)),
                   pl.BlockSpec((1, t, gate_w), lambda bi, h, qi: (bi, qi, h))),
        scratch_shapes=[pltpu.VMEM((s // t, w, t), BF16),
                        pltpu.VMEM((2, SB_G, t, t), F32),
                        pltpu.VMEM((2, SB_G, 2 * t, t), BF16),
                        pltpu.VMEM((2, SB_G, t, t), F32),
                        pltpu.VMEM((SB_G, HEAD_DIM), F32),
                        pltpu.VMEM((t, gate_w), F32)],
        compiler_params=pltpu.CompilerParams(
            dimension_semantics=("arbitrary", "arbitrary", "arbitrary"),
            vmem_limit_bytes=VMEM_LIMIT),
        name="sb_attn",
    )(proj3d, proj3d, proj3d, xn3d, w_in_bf)


DF_T = 256
DF_SLOPES = [2.0 ** (-8.0 * (h + 1) / DF_HEADS) for h in range(DF_HEADS)]
DF_V_AUG = DF_V_DIM + 16
DF_SLOPE_PIECES = 4


def _bf16_pieces(x, n):
    pieces = []
    for _ in range(n):
        p = float(np.asarray(x, dtype=np.float32).astype(jnp.bfloat16).astype(np.float32))
        pieces.append(p)
        x = x - p
    return pieces


def _df_attn_kernel(lamp_ref, q_ref, k_ref, v_ref, sg_ref, wg_ref, wu_ref, wd_ref,
                    o_ref, wg_out_ref, wu_out_ref, wd_out_ref,
                    bias_ref, kc_ref, qc_ref, acc_ref, vt_ref, p_ref, al_ref, m_ref):
    bi = pl.program_id(0)
    qi = pl.program_id(1)
    t = DF_T
    va = DF_V_AUG

    wg_out_ref[...] = wg_ref[...].astype(BF16)
    wu_out_ref[...] = wu_ref[...].astype(BF16)

    @pl.when((qi & 1) == 0)
    def _():
        wd_out_ref[...] = wd_ref[...].astype(BF16)

    @pl.when(qi == 0)
    def _():
        _fill_vt(v_ref, vt_ref, t, DF_V_DIM, va)

    @pl.when((bi == 0) & (qi == 0))
    def _():
        key = lax.broadcasted_iota(jnp.int32, (t, t), 0)
        qry = lax.broadcasted_iota(jnp.int32, (t, t), 1)
        rel = (qry - key).astype(F32)
        allowed = (key // CHUNK) <= (qry // CHUNK)
        pos = lax.broadcasted_iota(jnp.int32, (t, HEAD_DIM), 0).astype(F32)
        col = lax.broadcasted_iota(jnp.int32, (t, HEAD_DIM), 1)
        n = DF_SLOPE_PIECES
        for h in range(DF_HEADS):
            sl = DF_SLOPES[h] * LOG2E
            bias_ref[h] = jnp.where(allowed, -sl * jnp.abs(rel), NEG_BIG)
            pieces = jnp.zeros((t, HEAD_DIM), F32)
            for idx, piece in enumerate(_bf16_pieces(sl, n)):
                pieces = jnp.where((col == idx) | (col == n + idx), piece, pieces)
            kc_ref[h] = jnp.where(col < n, pos, jnp.where(col < 2 * n, pieces, 0.0)).astype(BF16)
            qc_ref[h] = jnp.where(col < n, pieces, jnp.where(col < 2 * n, -pos, 0.0)).astype(BF16)

    chains = [(h, c) for h in range(DF_HEADS) for c in range(2)]

    def key_block(j):
        return jnp.where(j == 0, qi, j - 1)

    def s1_matmul(j, diagonal):
        start = pl.multiple_of(key_block(j) * t, t)
        out = []
        for h, c in chains:
            lanes = slice((2 * h + c) * HEAD_DIM, (2 * h + c + 1) * HEAD_DIM)
            kb, qb = k_ref[0, pl.ds(start, t), lanes], q_ref[0, :, lanes]
            if diagonal:
                out.append(_dot_nt(kb, qb) + bias_ref[h])
            else:
                out.append(_dot_nt(jnp.concatenate([kb, kc_ref[h]], axis=1),
                                   jnp.concatenate([qb, qc_ref[h]], axis=1)))
        return out

    def s1_finish(s, j, slot):
        dist = ((qi - key_block(j)) * t).astype(F32)
        for i, (h, c) in enumerate(chains):
            row = slice(i, i + 1)
            shift = -(DF_SLOPES[h] * LOG2E) * dist
            m = m_ref[row, :]
            m_new = jnp.maximum(m, jnp.max(s[i], axis=0, keepdims=True) + shift)
            m_ref[row, :] = m_new
            al_ref[slot, row, :] = jnp.exp2(m - m_new)
            p_ref[slot, i] = jnp.exp2(s[i] - (m_new - shift)).astype(BF16)

    def s2_matmul(j, slot):
        kb = key_block(j)
        return [_dot(vt_ref[kb, h * va:(h + 1) * va, :], p_ref[slot, i])
                for i, (h, c) in enumerate(chains)]

    def s2_finish(pv, slot):
        for i, (h, c) in enumerate(chains):
            acc_ref[h, c] = al_ref[slot, i:i + 1, :] * acc_ref[h, c] + pv[i]

    def step(j, slot):
        s = s1_matmul(j, False)
        pv = s2_matmul(j - 1, 1 - slot)
        s2_finish(pv, 1 - slot)
        s1_finish(s, j, slot)

    acc_ref[...] = jnp.zeros_like(acc_ref)
    m_ref[...] = jnp.full_like(m_ref, NEG_BIG)
    s1_finish(s1_matmul(0, True), 0, 0)

    def body(k, carry):
        step(2 * k + 1, 1)
        step(2 * k + 2, 0)
        return carry

    lax.fori_loop(0, qi // 2, body, 0)

    @pl.when((qi & 1) == 1)
    def _():
        step(qi, 1)
        s2_finish(s2_matmul(qi, 1), 1)

    @pl.when((qi & 1) == 0)
    def _():
        s2_finish(s2_matmul(qi, 0), 0)

    lp = lamp_ref[...]
    lam = (jnp.exp(jnp.sum(lp[0:1] * lp[1:2], axis=-1, keepdims=True))
           - jnp.exp(jnp.sum(lp[2:3] * lp[3:4], axis=-1, keepdims=True)) + LAMBDA_INIT)
    for h in range(DF_HEADS):
        a1, a2 = acc_ref[h, 0], acc_ref[h, 1]
        l1, l2 = a1[DF_V_DIM:DF_V_DIM + 1, :], a2[DF_V_DIM:DF_V_DIM + 1, :]
        out_t = a1[0:DF_V_DIM, :] / l1 - lam * (a2[0:DF_V_DIM, :] / l2)
        ms = jnp.mean(out_t * out_t, axis=0, keepdims=True)
        out = (out_t * lax.rsqrt(ms + SUBLN_EPS)).T * (sg_ref[...] * (1.0 - LAMBDA_INIT))
        o_ref[0, :, h * DF_V_DIM:(h + 1) * DF_V_DIM] = out.astype(o_ref.dtype)


def _df_attn(proj3d, lam_params, subln_g, wg, wu, wd):
    b, s, _ = proj3d.shape
    t = DF_T
    nq = s // t
    grid = (b, nq)
    steps = b * nq
    cq, ck, cv = OFF_DFQ // DF_QK_WIDTH, OFF_DFK // DF_QK_WIDTH, OFF_DFV // DF_V_WIDTH
    single = pl.Buffered(1)
    d, d_ff = wg.shape
    wg3 = wg.reshape(steps, d // steps, d_ff)
    wu3 = wu.reshape(steps, d // steps, d_ff)
    wd3 = wd.reshape(steps // 2, 2 * d_ff // steps, d)
    gu_spec = pl.BlockSpec((1,) + wg3.shape[1:], lambda bi, qi: (bi * nq + qi, 0, 0))
    dn_spec = pl.BlockSpec((1,) + wd3.shape[1:], lambda bi, qi: ((bi * nq + qi) // 2, 0, 0))
    out_b, wg_bf, wu_bf, wd_bf = pl.pallas_call(
        _df_attn_kernel,
        out_shape=(jax.ShapeDtypeStruct((b, s, DF_V_WIDTH), BF16),
                   jax.ShapeDtypeStruct(wg3.shape, BF16),
                   jax.ShapeDtypeStruct(wu3.shape, BF16),
                   jax.ShapeDtypeStruct(wd3.shape, BF16)),
        grid=grid,
        in_specs=[
            pl.BlockSpec((4, HEAD_DIM), lambda bi, qi: (0, 0)),
            pl.BlockSpec((1, t, DF_QK_WIDTH), lambda bi, qi: (bi, qi, cq)),
            pl.BlockSpec((1, s, DF_QK_WIDTH), lambda bi, qi: (bi, 0, ck), pipeline_mode=single),
            pl.BlockSpec((1, s, DF_V_WIDTH), lambda bi, qi: (bi, 0, cv), pipeline_mode=single),
            pl.BlockSpec((1, DF_V_DIM), lambda bi, qi: (0, 0)),
            gu_spec, gu_spec, dn_spec,
        ],
        out_specs=(pl.BlockSpec((1, t, DF_V_WIDTH), lambda bi, qi: (bi, qi, 0)),
                   gu_spec, gu_spec, dn_spec),
        scratch_shapes=[pltpu.VMEM((DF_HEADS, t, t), F32),
                        pltpu.VMEM((DF_HEADS, t, HEAD_DIM), BF16),
                        pltpu.VMEM((DF_HEADS, t, HEAD_DIM), BF16),
                        pltpu.VMEM((DF_HEADS, 2, DF_V_AUG, t), F32),
                        pltpu.VMEM((s // t, DF_HEADS * DF_V_AUG, t), BF16),
                        pltpu.VMEM((2, 2 * DF_HEADS, t, t), BF16),
                        pltpu.VMEM((2, 2 * DF_HEADS, t), F32),
                        pltpu.VMEM((2 * DF_HEADS, t), F32)],
        compiler_params=pltpu.CompilerParams(
            dimension_semantics=("arbitrary", "arbitrary"),
            vmem_limit_bytes=VMEM_LIMIT),
        name="df_attn",
    )(lam_params, proj3d, proj3d, proj3d, subln_g, wg3, wu3, wd3)
    return out_b, wg_bf.reshape(wg.shape), wu_bf.reshape(wu.shape), wd_bf.reshape(wd.shape)


def _merge_kernel(a_ref, wa_ref, b_ref, wb_ref, ga_ref, gb_ref, o_ref):
    pa = _dot(a_ref[...], wa_ref[...])
    pb = _dot(b_ref[...], wb_ref[...])
    ga = jax.nn.sigmoid(ga_ref[...].astype(F32))
    gb = jax.nn.sigmoid(gb_ref[...].astype(F32))
    o_ref[...] = (ga * pa + gb * pb).astype(o_ref.dtype)


def _merge(out_a, wa, out_b, wb, gates, tm=1024, tn=1024):
    t = out_a.shape[0]
    grid = (t // tm, D_MODEL // tn)
    ca, cb = 0, D_MODEL // tn
    return pl.pallas_call(
        _merge_kernel,
        out_shape=jax.ShapeDtypeStruct((t, D_MODEL), BF16),
        grid=grid,
        in_specs=[
            pl.BlockSpec((tm, SB_WIDTH), lambda i, j: (i, 0)),
            pl.BlockSpec((SB_WIDTH, tn), lambda i, j: (0, j)),
            pl.BlockSpec((tm, DF_V_WIDTH), lambda i, j: (i, 0)),
            pl.BlockSpec((DF_V_WIDTH, tn), lambda i, j: (0, j)),
            pl.BlockSpec((tm, tn), lambda i, j: (i, ca + j)),
            pl.BlockSpec((tm, tn), lambda i, j: (i, cb + j)),
        ],
        out_specs=pl.BlockSpec((tm, tn), lambda i, j: (i, j)),
        compiler_params=pltpu.CompilerParams(
            dimension_semantics=("arbitrary", "arbitrary"),
            vmem_limit_bytes=VMEM_LIMIT),
        name="merge",
    )(out_a, wa, out_b, wb, gates, gates)


def _out_proj_kernel(m_ref, w_ref, x_ref, g_ref, h_ref, hn_ref):
    h = x_ref[...] + _dot(m_ref[...], w_ref[...])
    h_ref[...] = h
    ms = jnp.mean(h * h, axis=-1, keepdims=True)
    hn_ref[...] = (h * lax.rsqrt(ms + EPS) * g_ref[...]).astype(hn_ref.dtype)


def _out_proj(merged, w_out, x2d, g2, tm=512):
    t = merged.shape[0]
    grid = (t // tm,)
    return pl.pallas_call(
        _out_proj_kernel,
        out_shape=(jax.ShapeDtypeStruct((t, D_MODEL), F32),
                   jax.ShapeDtypeStruct((t, D_MODEL), BF16)),
        grid=grid,
        in_specs=[
            pl.BlockSpec((tm, D_MODEL), lambda i: (i, 0)),
            pl.BlockSpec((D_MODEL, D_MODEL), lambda i: (0, 0)),
            pl.BlockSpec((tm, D_MODEL), lambda i: (i, 0)),
            pl.BlockSpec((1, D_MODEL), lambda i: (0, 0)),
        ],
        out_specs=(pl.BlockSpec((tm, D_MODEL), lambda i: (i, 0)),
                   pl.BlockSpec((tm, D_MODEL), lambda i: (i, 0))),
        compiler_params=pltpu.CompilerParams(
            dimension_semantics=("arbitrary",),
            vmem_limit_bytes=VMEM_LIMIT),
        name="out_proj",
    )(merged, w_out, x2d, g2)


FFN_SUB = 2


def _ffn_kernel(hn_ref, h_ref, wg_ref, wu_ref, wd_ref, o_ref):
    j = pl.program_id(1)

    @pl.when(j == 0)
    def _():
        o_ref[...] = h_ref[...]

    hn = hn_ref[...]
    tf = wg_ref.shape[1]
    sub = tf // FFN_SUB
    gu = []
    for c in range(FFN_SUB):
        cols = slice(c * sub, (c + 1) * sub)
        gu.append((_dot(hn, wg_ref[:, cols]), _dot(hn, wu_ref[:, cols])))
    acc = None
    for c in range(FFN_SUB):
        g, u = gu[c]
        hid = (g * jax.nn.sigmoid(g) * u).astype(BF16)
        d = _dot(hid, wd_ref[c * sub:(c + 1) * sub, :])
        acc = d if acc is None else acc + d
    o_ref[...] += acc


def _ffn(hn, h, wg, wu, wd, tm=512, tf=512):
    t = hn.shape[0]
    d_ff = wg.shape[1]
    grid = (t // tm, d_ff // tf)
    return pl.pallas_call(
        _ffn_kernel,
        out_shape=jax.ShapeDtypeStruct((t, D_MODEL), F32),
        grid=grid,
        in_specs=[
            pl.BlockSpec((tm, D_MODEL), lambda i, j: (i, 0)),
            pl.BlockSpec((tm, D_MODEL), lambda i, j: (i, 0)),
            pl.BlockSpec((D_MODEL, tf), lambda i, j: (0, j)),
            pl.BlockSpec((D_MODEL, tf), lambda i, j: (0, j)),
            pl.BlockSpec((tf, D_MODEL), lambda i, j: (j, 0)),
        ],
        out_specs=pl.BlockSpec((tm, D_MODEL), lambda i, j: (i, 0)),
        compiler_params=pltpu.CompilerParams(
            dimension_semantics=("arbitrary", "arbitrary"),
            vmem_limit_bytes=VMEM_LIMIT),
        name="ffn",
    )(hn, h, wg, wu, wd)


def kernel(x, norm1_g, w_in, q_norm_g, k_norm_g, lambda_q1, lambda_k1, lambda_q2, lambda_k2,
           subln_g, w_branch_a, w_branch_b, w_out, norm2_g, w_ffn_gate, w_ffn_up, w_ffn_down):
    b, s, d = x.shape
    t = b * s
    layer = 0
    x2d = x.reshape(t, d)
    lam_params = jnp.stack([lambda_q1[layer], lambda_k1[layer], lambda_q2[layer], lambda_k2[layer]])

    w_in_bf = w_in[layer].astype(BF16)
    proj, xn = _in_proj(x2d, norm1_g[layer].reshape(1, d), w_in_bf,
                        q_norm_g[layer].reshape(1, HEAD_DIM), k_norm_g[layer].reshape(1, HEAD_DIM))
    proj3d = proj.reshape(b, s, OFF_GA)

    out_a, gates = _sb_attn(proj3d, xn.reshape(b, s, d), w_in_bf)
    out_a = out_a.reshape(t, SB_WIDTH)
    out_b, wg, wu, wd = _df_attn(proj3d, lam_params, subln_g[layer].reshape(1, DF_V_DIM),
                                 w_ffn_gate[layer], w_ffn_up[layer], w_ffn_down[layer])
    out_b = out_b.reshape(t, DF_V_WIDTH)

    merged = _merge(out_a, w_branch_a[layer].astype(BF16), out_b, w_branch_b[layer].astype(BF16),
                    gates.reshape(t, 2 * d))
    h, hn = _out_proj(merged, w_out[layer].astype(BF16), x2d, norm2_g[layer].reshape(1, d))
    out = _ffn(hn, h, wg, wu, wd)
    return out.reshape(b, s, d)
```

```python
import math

import numpy as np
import jax
import jax.numpy as jnp
from jax import lax
from jax.experimental import pallas as pl
from jax.experimental.pallas import tpu as pltpu

F32 = jnp.float32
BF16 = jnp.bfloat16

D_MODEL = 2048
SB_HEADS = 8
HEAD_DIM = 128
DF_HEADS = 4
DF_V_DIM = 256
CHUNK = 64
SB_WIDTH = SB_HEADS * HEAD_DIM
DF_QK_WIDTH = DF_HEADS * 2 * HEAD_DIM
DF_V_WIDTH = DF_HEADS * DF_V_DIM
IN_WIDTH = 3 * SB_WIDTH + 2 * DF_QK_WIDTH + DF_V_WIDTH + 2 * D_MODEL
EPS = 1e-6
SUBLN_EPS = 1e-5
LAMBDA_INIT = 0.8 - 0.6 * math.exp(-0.3 * 0)
LOG2E = math.log2(math.e)
Q_SCALE = LOG2E / math.sqrt(HEAD_DIM)

OFF_SBQ = 0
OFF_SBK = SB_WIDTH
OFF_SBV = 2 * SB_WIDTH
OFF_DFQ = 3 * SB_WIDTH
OFF_DFK = OFF_DFQ + DF_QK_WIDTH
OFF_DFV = OFF_DFK + DF_QK_WIDTH
OFF_GA = OFF_DFV + DF_V_WIDTH
OFF_GB = OFF_GA + D_MODEL

VMEM_LIMIT = 56 * 1024 * 1024
NEG_BIG = -1e30


def _dot(a, b):
    return jnp.dot(a, b, preferred_element_type=F32)


def _dot_nt(a, b):
    return lax.dot_general(a, b, (((1,), (1,)), ((), ())), preferred_element_type=F32)


PROJ_TN = 1024


def _in_proj_kernel(x_ref, g_ref, w_ref, qg_ref, kg_ref, o_ref, xn_ref):
    j = pl.program_id(1)

    @pl.when(j == 0)
    def _():
        x = x_ref[...]
        ms = jnp.mean(x * x, axis=-1, keepdims=True)
        xn_ref[...] = (x * lax.rsqrt(ms + EPS) * g_ref[...]).astype(BF16)

    is_sbq = j == OFF_SBQ // PROJ_TN
    is_q = j == OFF_DFQ // PROJ_TN
    is_qk = is_q | (j == OFF_DFK // PROJ_TN)

    def proj():
        return _dot(xn_ref[...], w_ref[...])

    @pl.when(jnp.logical_not(is_qk))
    def _():
        o_ref[...] = (proj() * jnp.where(is_sbq, Q_SCALE, 1.0)).astype(o_ref.dtype)

    @pl.when(is_qk)
    def _():
        acc = proj()
        gain = jnp.where(is_q, qg_ref[...] * Q_SCALE, kg_ref[...])
        for c in range(PROJ_TN // HEAD_DIM):
            a = acc[:, c * HEAD_DIM:(c + 1) * HEAD_DIM]
            ms = jnp.mean(a * a, axis=-1, keepdims=True)
            o_ref[:, c * HEAD_DIM:(c + 1) * HEAD_DIM] = (
                a * lax.rsqrt(ms + EPS) * gain).astype(o_ref.dtype)


def _in_proj(x2d, g1, w_in_bf, qg, kg, tm=1024):
    t = x2d.shape[0]
    grid = (t // tm, OFF_GA // PROJ_TN)
    return pl.pallas_call(
        _in_proj_kernel,
        out_shape=(jax.ShapeDtypeStruct((t, OFF_GA), BF16),
                   jax.ShapeDtypeStruct((t, D_MODEL), BF16)),
        grid=grid,
        in_specs=[
            pl.BlockSpec((tm, D_MODEL), lambda i, j: (i, 0)),
            pl.BlockSpec((1, D_MODEL), lambda i, j: (0, 0)),
            pl.BlockSpec((D_MODEL, PROJ_TN), lambda i, j: (0, j)),
            pl.BlockSpec((1, HEAD_DIM), lambda i, j: (0, 0)),
            pl.BlockSpec((1, HEAD_DIM), lambda i, j: (0, 0)),
        ],
        out_specs=(pl.BlockSpec((tm, PROJ_TN), lambda i, j: (i, j)),
                   pl.BlockSpec((tm, D_MODEL), lambda i, j: (i, 0))),
        compiler_params=pltpu.CompilerParams(
            dimension_semantics=("arbitrary", "arbitrary"),
            vmem_limit_bytes=VMEM_LIMIT),
        name="in_proj",
    )(x2d, g1, w_in_bf, qg, kg)


SB_T = 256
SB_G = 4


def _softplus2(a):
    return jnp.maximum(a, 0.0) + jnp.log2(1.0 + jnp.exp2(jnp.minimum(a, -a)))


def _split_hi_lo(a):
    hi = a.astype(BF16)
    lo = (a - hi.astype(F32)).astype(BF16)
    return hi, lo


def _eye(n):
    return (lax.broadcasted_iota(jnp.int32, (n, n), 0)
            == lax.broadcasted_iota(jnp.int32, (n, n), 1)).astype(BF16)


def _fill_vt(v_ref, vt_ref, t):
    eye = _eye(HEAD_DIM)
    n_groups = v_ref.shape[2] // HEAD_DIM

    def body(c, _):
        start = pl.multiple_of(c * t, t)
        for g in range(n_groups):
            lanes = slice(g * HEAD_DIM, (g + 1) * HEAD_DIM)
            vt_ref[c, lanes, :] = _dot_nt(eye, v_ref[0, pl.ds(start, t), lanes]).astype(BF16)
        return 0

    lax.fori_loop(0, v_ref.shape[1] // t, body, 0)


SB_EXIT_LOG2 = 160.0
NORM_SLACK = 1.01


def _max_key_norm2(k_ref, kn_ref, t):
    ones8 = jnp.ones((8, HEAD_DIM), BF16)
    n_groups = k_ref.shape[2] // HEAD_DIM

    def body(c, mx):
        start = pl.multiple_of(c * t, t)
        out = []
        for g in range(n_groups):
            kf = k_ref[0, pl.ds(start, t), g * HEAD_DIM:(g + 1) * HEAD_DIM].astype(F32)
            out.append(jnp.maximum(mx[g], _dot_nt(ones8, (kf * kf).astype(BF16))))
        return tuple(out)

    mx = lax.fori_loop(0, k_ref.shape[1] // t, body,
                       tuple(jnp.zeros((8, t), F32) for _ in range(n_groups)))
    for g in range(n_groups):
        kn_ref[g:g + 1, :] = jnp.broadcast_to(
            jnp.max(mx[g][0:1, :], axis=1, keepdims=True), (1, HEAD_DIM))


GATE_CHUNKS = 8
GATE_AFTER_QK = 6
GATE_AFTER_CUM = 2


def _sb_attn_kernel(q_ref, k_ref, v_ref, xn_ref, wgate_ref, o_ref, gate_ref,
                    vt_ref, a_ref, hl_ref, e_ref, kn_ref, gbuf_ref):
    qi = pl.program_id(2)
    t = SB_T
    heads = range(SB_G)

    @pl.when(qi == 0)
    def _():
        _fill_vt(v_ref, vt_ref, t)
        _max_key_norm2(k_ref, kn_ref, t)

    r2 = lax.broadcasted_iota(jnp.int32, (t, 2 * t), 0)
    c2 = lax.broadcasted_iota(jnp.int32, (t, 2 * t), 1)
    l2 = ((c2 & (t - 1)) >= r2).astype(BF16)
    lanes = [slice(g * HEAD_DIM, (g + 1) * HEAD_DIM) for g in heads]

    def block_start(j):
        return pl.multiple_of(jnp.maximum(qi - j, 0) * t, t)

    def strict_mask():
        key = lax.broadcasted_iota(jnp.int32, (t, t), 0)
        qry = lax.broadcasted_iota(jnp.int32, (t, t), 1)
        return key < qry

    def s1_matmul(j):
        start = block_start(j)
        return [_dot_nt(k_ref[0, pl.ds(start, t), lanes[g]], q_ref[0, :, lanes[g]]) for g in heads]

    def s1_finish(a, slot, masked):
        for g in heads:
            sp = _softplus2(a[g])
            if masked:
                sp = jnp.where(strict_mask(), sp, 0.0)
            hi, lo = _split_hi_lo(sp)
            a_ref[slot, g] = a[g]
            hl_ref[slot, g, 0:t, :] = hi
            hl_ref[slot, g, t:2 * t, :] = lo

    def s2_matmul(slot):
        return [_dot(l2, hl_ref[slot, g]) for g in heads]

    def s2_finish(cum, slot, masked):
        for g in heads:
            e0 = a_ref[slot, g] - cum[g]
            if masked:
                e0 = jnp.where(strict_mask(), e0, NEG_BIG)
            e_ref[slot, g] = e0
        return tuple(c[0:1, :] for c in cum)

    def s3_weights(slot, carry):
        return [jnp.exp2(e_ref[slot, g] - carry[g]).astype(BF16) for g in heads]

    def s3_matmul(j, w):
        kb = jnp.maximum(qi - j, 0)
        return [_dot(vt_ref[kb, lanes[g], :], w[g]) for g in heads]

    def step(i, slot, st):
        acc, carry, sums = st
        w = s3_weights(slot, carry)
        a = s1_matmul(i)
        cum = s2_matmul(1 - slot)
        pv = s3_matmul(i - 2, w)
        s1_finish(a, slot, False)
        new_sums = s2_finish(cum, 1 - slot, False)
        return (tuple(acc[g] + pv[g] for g in heads),
                tuple(carry[g] + sums[g] for g in heads), new_sums)

    ones8 = jnp.ones((8, HEAD_DIM), BF16)
    bound = []
    for g in heads:
        qf = q_ref[0, :, lanes[g]].astype(F32)
        qn2 = _dot_nt(ones8, (qf * qf).astype(BF16))[0:1, :]
        bound.append(jnp.sqrt(qn2 * kn_ref[g:g + 1, 0:1]) * NORM_SLACK)

    def all_underflow(carry):
        slack = bound[0] - carry[0]
        for g in heads[1:]:
            slack = jnp.maximum(slack, bound[g] - carry[g])
        return jnp.max(slack) < -SB_EXIT_LOG2

    gate_cols = wgate_ref.shape[1] // GATE_CHUNKS

    def gate_chunk(c):
        cols = slice(c * gate_cols, (c + 1) * gate_cols)
        gbuf_ref[:, cols] = _dot(xn_ref[0], wgate_ref[:, cols])

    a0 = s1_matmul(0)
    a1 = s1_matmul(1)
    for c in range(0, GATE_AFTER_QK):
        gate_chunk(c)
    s1_finish(a0, 0, True)
    s1_finish(a1, 1, False)
    cum0 = s2_matmul(0)
    cum1 = s2_matmul(1)
    for c in range(GATE_AFTER_QK, GATE_AFTER_QK + GATE_AFTER_CUM):
        gate_chunk(c)
    sums0 = s2_finish(cum0, 0, True)
    sums1 = s2_finish(cum1, 1, False)
    w0 = s3_weights(0, tuple(jnp.zeros((1, t), F32) for _ in heads))
    w1 = s3_weights(1, sums0)
    pv0 = s3_matmul(0, w0)
    pv1 = s3_matmul(1, w1)
    for c in range(GATE_AFTER_QK + GATE_AFTER_CUM, GATE_CHUNKS):
        gate_chunk(c)
    gate_ref[0] = gbuf_ref[...].astype(gate_ref.dtype)
    acc2 = tuple(pv0[g] + jnp.where(qi > 0, pv1[g], 0.0) for g in heads)
    carry2 = tuple(sums0[g] + sums1[g] for g in heads)

    def remaining_blocks(acc, carry):
        s1_finish(s1_matmul(2), 0, False)
        a = s1_matmul(3)
        cum = s2_matmul(0)
        s1_finish(a, 1, False)
        sums = s2_finish(cum, 0, False)
        pairs = (qi - 2) // 2

        def cond(st):
            k, done = st[0], st[1]
            return (k < pairs) & jnp.logical_not(done)

        def body(st):
            k = st[0]
            acc, carry, sums = step(2 * k + 5, 1, step(2 * k + 4, 0, st[2:]))
            return (k + 1, all_underflow(carry), acc, carry, sums)

        k, _, acc, carry, sums = lax.while_loop(
            cond, body, (jnp.int32(0), False, acc, carry, sums))

        w = s3_weights(0, carry)
        cum = s2_matmul(1)
        pv = s3_matmul(2 * k + 2, w)
        s2_finish(cum, 1, False)
        acc = tuple(acc[g] + pv[g] for g in heads)
        carry = tuple(carry[g] + sums[g] for g in heads)
        pv = s3_matmul(2 * k + 3, s3_weights(1, carry))
        last_is_real = 2 * k + 3 <= qi
        return tuple(acc[g] + jnp.where(last_is_real, pv[g], 0.0) for g in heads)

    more = (qi >= 2) & jnp.logical_not(all_underflow(carry2))
    acc = lax.cond(more, lambda: remaining_blocks(acc2, carry2), lambda: acc2)
    for g in heads:
        o_ref[0, :, lanes[g]] = acc[g].T.astype(o_ref.dtype)


def _sb_attn(proj3d, xn3d, w_in_bf):
    b, s, _ = proj3d.shape
    t = SB_T
    w = SB_G * HEAD_DIM
    groups = SB_HEADS // SB_G
    grid = (b, groups, s // t)
    cq, ck, cv = OFF_SBQ // w, OFF_SBK // w, OFF_SBV // w
    gate_w = 2 * D_MODEL // groups
    cg = OFF_GA // gate_w
    single = pl.Buffered(1)
    return pl.pallas_call(
        _sb_attn_kernel,
        out_shape=(jax.ShapeDtypeStruct((b, s, SB_WIDTH), BF16),
                   jax.ShapeDtypeStruct((b, s, 2 * D_MODEL), BF16)),
        grid=grid,
        in_specs=[
            pl.BlockSpec((1, t, w), lambda bi, h, qi: (bi, qi, cq + h)),
            pl.BlockSpec((1, s, w), lambda bi, h, qi: (bi, 0, ck + h)),
            pl.BlockSpec((1, s, w), lambda bi, h, qi: (bi, 0, cv + h)),
            pl.BlockSpec((1, t, D_MODEL), lambda bi, h, qi: (bi, qi, 0)),
            pl.BlockSpec((D_MODEL, gate_w), lambda bi, h, qi: (0, cg + h), pipeline_mode=single),
        ],
        out_specs=(pl.BlockSpec((1, t, w), lambda bi, h, qi: (bi, qi, h)),
                   pl.BlockSpec((1, t, gate_w), lambda bi, h, qi: (bi, qi, h))),
        scratch_shapes=[pltpu.VMEM((s // t, w, t), BF16),
                        pltpu.VMEM((2, SB_G, t, t), F32),
                        pltpu.VMEM((2, SB_G, 2 * t, t), BF16),
                        pltpu.VMEM((2, SB_G, t, t), F32),
                        pltpu.VMEM((SB_G, HEAD_DIM), F32),
                        pltpu.VMEM((t, gate_w), F32)],
        compiler_params=pltpu.CompilerParams(
            dimension_semantics=("arbitrary", "arbitrary", "arbitrary"),
            vmem_limit_bytes=VMEM_LIMIT),
        name="sb_attn",
    )(proj3d, proj3d, proj3d, xn3d, w_in_bf)


DF_T = 256
DF_SLOPES = [2.0 ** (-8.0 * (h + 1) / DF_HEADS) for h in range(DF_HEADS)]
DF_V_AUG = DF_V_DIM + 16
DF_SLOPE_PIECES = 4


def _bf16_pieces(x, n):
    pieces = []
    for _ in range(n):
        p = float(np.asarray(x, dtype=np.float32).astype(jnp.bfloat16).astype(np.float32))
        pieces.append(p)
        x = x - p
    return pieces


def _df_attn_kernel(lamp_ref, q_ref, k_ref, vt_ref, sg_ref, wg_ref, wu_ref, wd_ref,
                    o_ref, wg_out_ref, wu_out_ref, wd_out_ref,
                    bias_ref, kc_ref, qc_ref, acc_ref, p_ref, al_ref, m_ref):
    bi = pl.program_id(0)
    qs = pl.program_id(1)
    t = DF_T
    va = DF_V_AUG

    wg_out_ref[...] = wg_ref[...].astype(BF16)
    wu_out_ref[...] = wu_ref[...].astype(BF16)
    wd_out_ref[...] = wd_ref[...].astype(BF16)

    @pl.when((bi == 0) & (qs == 0))
    def _():
        key = lax.broadcasted_iota(jnp.int32, (t, t), 0)
        qry = lax.broadcasted_iota(jnp.int32, (t, t), 1)
        rel = (qry - key).astype(F32)
        allowed = (key // CHUNK) <= (qry // CHUNK)
        pos = lax.broadcasted_iota(jnp.int32, (t, HEAD_DIM), 0).astype(F32)
        col = lax.broadcasted_iota(jnp.int32, (t, HEAD_DIM), 1)
        n = DF_SLOPE_PIECES
        for h in range(DF_HEADS):
            sl = DF_SLOPES[h] * LOG2E
            bias_ref[h] = jnp.where(allowed, -sl * jnp.abs(rel), NEG_BIG)
            pieces = jnp.zeros((t, HEAD_DIM), F32)
            for idx, piece in enumerate(_bf16_pieces(sl, n)):
                pieces = jnp.where((col == idx) | (col == n + idx), piece, pieces)
            kc_ref[h] = jnp.where(col < n, pos, jnp.where(col < 2 * n, pieces, 0.0)).astype(BF16)
            qc_ref[h] = jnp.where(col < n, pieces, jnp.where(col < 2 * n, -pos, 0.0)).astype(BF16)

    all_chains = [(u, h, c) for u in range(2) for h in range(DF_HEADS) for c in range(2)]
    u1_chains = [ch for ch in all_chains if ch[0] == 1]

    def chain_id(ch):
        u, h, c = ch
        return (u * DF_HEADS + h) * 2 + c

    def lanes(h, c):
        return slice((2 * h + c) * HEAD_DIM, (2 * h + c + 1) * HEAD_DIM)

    def q_blk(u, h, c):
        return q_ref[0, u * t:(u + 1) * t, lanes(h, c)]

    def k_blk(kb, h, c):
        return k_ref[0, pl.ds(pl.multiple_of(kb * t, t), t), lanes(h, c)]

    def s1_offdiag(kb, chains):
        return [_dot_nt(jnp.concatenate([k_blk(kb, h, c), kc_ref[h]], axis=1),
                        jnp.concatenate([q_blk(u, h, c), qc_ref[h]], axis=1))
                for u, h, c in chains]

    def s1_diag():
        return [_dot_nt(k_blk(2 * qs + u, h, c), q_blk(u, h, c)) + bias_ref[h]
                for u, h, c in all_chains]

    def s1_finish(s, chains, blocks_away, slot):
        for s_i, ch in zip(s, chains):
            i = chain_id(ch)
            row = slice(i, i + 1)
            shift = -(DF_SLOPES[ch[1]] * LOG2E) * (blocks_away(ch[0]) * t).astype(F32)
            m = m_ref[row, :]
            m_new = jnp.maximum(m, jnp.max(s_i, axis=0, keepdims=True) + shift)
            m_ref[row, :] = m_new
            al_ref[slot, row, :] = jnp.exp2(m - m_new)
            p_ref[slot, i] = jnp.exp2(s_i - (m_new - shift)).astype(BF16)

    def s2_matmul(chains, key_block, slot):
        return [_dot(vt_ref[0, key_block(u), h * va:(h + 1) * va, :], p_ref[slot, chain_id((u, h, c))])
                for u, h, c in chains]

    def s2_finish(pv, chains, slot):
        for pv_i, ch in zip(pv, chains):
            i = chain_id(ch)
            u, h, c = ch
            acc_ref[u, h, c] = al_ref[slot, i:i + 1, :] * acc_ref[u, h, c] + pv_i

    acc_ref[...] = jnp.zeros_like(acc_ref)
    m_ref[...] = jnp.full_like(m_ref, NEG_BIG)

    s1_finish(s1_offdiag(2 * qs, u1_chains), u1_chains, lambda u: jnp.int32(1), 0)

    s = s1_diag()
    pv = s2_matmul(u1_chains, lambda u: 2 * qs, 0)
    s2_finish(pv, u1_chains, 0)
    s1_finish(s, all_chains, lambda u: jnp.int32(0), 1)

    def common_step(kb, slot, prev_block):
        s = s1_offdiag(kb, all_chains)
        pv = s2_matmul(all_chains, prev_block, 1 - slot)
        s2_finish(pv, all_chains, 1 - slot)
        s1_finish(s, all_chains, lambda u: 2 * qs + u - kb, slot)

    def body(k, carry):
        common_step(2 * k, 0, lambda u: jnp.where(k == 0, 2 * qs + u, 2 * k - 1))
        common_step(2 * k + 1, 1, lambda u: 2 * k)
        return carry

    lax.fori_loop(0, qs, body, 0)

    pv = s2_matmul(all_chains, lambda u: jnp.where(qs == 0, 2 * qs + u, 2 * qs - 1), 1)
    s2_finish(pv, all_chains, 1)

    lp = lamp_ref[...]
    lam = (jnp.exp(jnp.sum(lp[0:1] * lp[1:2], axis=-1, keepdims=True))
           - jnp.exp(jnp.sum(lp[2:3] * lp[3:4], axis=-1, keepdims=True)) + LAMBDA_INIT)
    for u in range(2):
        for h in range(DF_HEADS):
            a1, a2 = acc_ref[u, h, 0], acc_ref[u, h, 1]
            l1, l2 = a1[DF_V_DIM:DF_V_DIM + 1, :], a2[DF_V_DIM:DF_V_DIM + 1, :]
            out_t = a1[0:DF_V_DIM, :] / l1 - lam * (a2[0:DF_V_DIM, :] / l2)
            ms = jnp.mean(out_t * out_t, axis=0, keepdims=True)
            out = (out_t * lax.rsqrt(ms + SUBLN_EPS)).T * (sg_ref[...] * (1.0 - LAMBDA_INIT))
            o_ref[0, u * t:(u + 1) * t, h * DF_V_DIM:(h + 1) * DF_V_DIM] = out.astype(o_ref.dtype)


def _df_attn(proj3d, lam_params, subln_g, wg, wu, wd):
    b, s, _ = proj3d.shape
    t = DF_T
    tq = 2 * t
    nq = s // tq
    nkb = s // t
    grid = (b, nq)
    steps = b * nq
    cq, ck = OFF_DFQ // DF_QK_WIDTH, OFF_DFK // DF_QK_WIDTH
    single = pl.Buffered(1)
    v = proj3d[:, :, OFF_DFV:OFF_DFV + DF_V_WIDTH].reshape(b, nkb, t, DF_HEADS, DF_V_DIM)
    vt = jnp.concatenate([v.transpose(0, 1, 3, 4, 2),
                          jnp.ones((b, nkb, DF_HEADS, DF_V_AUG - DF_V_DIM, t), BF16)], axis=3)
    vt = vt.reshape(b, nkb, DF_HEADS * DF_V_AUG, t)
    d, d_ff = wg.shape
    wg3 = wg.reshape(steps, d // steps, d_ff)
    wu3 = wu.reshape(steps, d // steps, d_ff)
    wd3 = wd.reshape(steps, d_ff // steps, d)
    gu_spec = pl.BlockSpec((1,) + wg3.shape[1:], lambda bi, qs: (bi * nq + qs, 0, 0))
    dn_spec = pl.BlockSpec((1,) + wd3.shape[1:], lambda bi, qs: (bi * nq + qs, 0, 0))
    out_b, wg_bf, wu_bf, wd_bf = pl.pallas_call(
        _df_attn_kernel,
        out_shape=(jax.ShapeDtypeStruct((b, s, DF_V_WIDTH), BF16),
                   jax.ShapeDtypeStruct(wg3.shape, BF16),
                   jax.ShapeDtypeStruct(wu3.shape, BF16),
                   jax.ShapeDtypeStruct(wd3.shape, BF16)),
        grid=grid,
        in_specs=[
            pl.BlockSpec((4, HEAD_DIM), lambda bi, qs: (0, 0)),
            pl.BlockSpec((1, tq, DF_QK_WIDTH), lambda bi, qs: (bi, qs, cq)),
            pl.BlockSpec((1, s, DF_QK_WIDTH), lambda bi, qs: (bi, 0, ck), pipeline_mode=single),
            pl.BlockSpec((1, nkb, DF_HEADS * DF_V_AUG, t), lambda bi, qs: (bi, 0, 0, 0),
                         pipeline_mode=single),
            pl.BlockSpec((1, DF_V_DIM), lambda bi, qs: (0, 0)),
            gu_spec, gu_spec, dn_spec,
        ],
        out_specs=(pl.BlockSpec((1, tq, DF_V_WIDTH), lambda bi, qs: (bi, qs, 0)),
                   gu_spec, gu_spec, dn_spec),
        scratch_shapes=[pltpu.VMEM((DF_HEADS, t, t), F32),
                        pltpu.VMEM((DF_HEADS, t, HEAD_DIM), BF16),
                        pltpu.VMEM((DF_HEADS, t, HEAD_DIM), BF16),
                        pltpu.VMEM((2, DF_HEADS, 2, DF_V_AUG, t), F32),
                        pltpu.VMEM((2, 4 * DF_HEADS, t, t), BF16),
                        pltpu.VMEM((2, 4 * DF_HEADS, t), F32),
                        pltpu.VMEM((4 * DF_HEADS, t), F32)],
        compiler_params=pltpu.CompilerParams(
            dimension_semantics=("arbitrary", "arbitrary"),
            vmem_limit_bytes=VMEM_LIMIT),
        name="df_attn",
    )(lam_params, proj3d, proj3d, vt, subln_g, wg3, wu3, wd3)
    return out_b, wg_bf.reshape(wg.shape), wu_bf.reshape(wu.shape), wd_bf.reshape(wd.shape)


def _merge_kernel(a_ref, wa_ref, b_ref, wb_ref, ga_ref, gb_ref, o_ref):
    pa = _dot(a_ref[...], wa_ref[...])
    pb = _dot(b_ref[...], wb_ref[...])
    ga = jax.nn.sigmoid(ga_ref[...].astype(F32))
    gb = jax.nn.sigmoid(gb_ref[...].astype(F32))
    o_ref[...] = (ga * pa + gb * pb).astype(o_ref.dtype)


def _merge(out_a, wa, out_b, wb, gates, tm=1024, tn=1024):
    t = out_a.shape[0]
    grid = (t // tm, D_MODEL // tn)
    ca, cb = 0, D_MODEL // tn
    return pl.pallas_call(
        _merge_kernel,
        out_shape=jax.ShapeDtypeStruct((t, D_MODEL), BF16),
        grid=grid,
        in_specs=[
            pl.BlockSpec((tm, SB_WIDTH), lambda i, j: (i, 0)),
            pl.BlockSpec((SB_WIDTH, tn), lambda i, j: (0, j)),
            pl.BlockSpec((tm, DF_V_WIDTH), lambda i, j: (i, 0)),
            pl.BlockSpec((DF_V_WIDTH, tn), lambda i, j: (0, j)),
            pl.BlockSpec((tm, tn), lambda i, j: (i, ca + j)),
            pl.BlockSpec((tm, tn), lambda i, j: (i, cb + j)),
        ],
        out_specs=pl.BlockSpec((tm, tn), lambda i, j: (i, j)),
        compiler_params=pltpu.CompilerParams(
            dimension_semantics=("arbitrary", "arbitrary"),
            vmem_limit_bytes=VMEM_LIMIT),
        name="merge",
    )(out_a, wa, out_b, wb, gates, gates)


def _out_proj_kernel(m_ref, w_ref, x_ref, g_ref, h_ref, hn_ref):
    h = x_ref[...] + _dot(m_ref[...], w_ref[...])
    h_ref[...] = h
    ms = jnp.mean(h * h, axis=-1, keepdims=True)
    hn_ref[...] = (h * lax.rsqrt(ms + EPS) * g_ref[...]).astype(hn_ref.dtype)


def _out_proj(merged, w_out, x2d, g2, tm=512):
    t = merged.shape[0]
    grid = (t // tm,)
    return pl.pallas_call(
        _out_proj_kernel,
        out_shape=(jax.ShapeDtypeStruct((t, D_MODEL), F32),
                   jax.ShapeDtypeStruct((t, D_MODEL), BF16)),
        grid=grid,
        in_specs=[
            pl.BlockSpec((tm, D_MODEL), lambda i: (i, 0)),
            pl.BlockSpec((D_MODEL, D_MODEL), lambda i: (0, 0)),
            pl.BlockSpec((tm, D_MODEL), lambda i: (i, 0)),
            pl.BlockSpec((1, D_MODEL), lambda i: (0, 0)),
        ],
        out_specs=(pl.BlockSpec((tm, D_MODEL), lambda i: (i, 0)),
                   pl.BlockSpec((tm, D_MODEL), lambda i: (i, 0))),
        compiler_params=pltpu.CompilerParams(
            dimension_semantics=("arbitrary",),
            vmem_limit_bytes=VMEM_LIMIT),
        name="out_proj",
    )(merged, w_out, x2d, g2)


FFN_SUB = 2


def _ffn_kernel(hn_ref, h_ref, wg_ref, wu_ref, wd_ref, o_ref):
    j = pl.program_id(1)

    @pl.when(j == 0)
    def _():
        o_ref[...] = h_ref[...]

    hn = hn_ref[...]
    tf = wg_ref.shape[1]
    sub = tf // FFN_SUB
    gu = []
    for c in range(FFN_SUB):
        cols = slice(c * sub, (c + 1) * sub)
        gu.append((_dot(hn, wg_ref[:, cols]), _dot(hn, wu_ref[:, cols])))
    acc = None
    for c in range(FFN_SUB):
        g, u = gu[c]
        hid = (g * jax.nn.sigmoid(g) * u).astype(BF16)
        d = _dot(hid, wd_ref[c * sub:(c + 1) * sub, :])
        acc = d if acc is None else acc + d
    o_ref[...] += acc


def _ffn(hn, h, wg, wu, wd, tm=512, tf=512):
    t = hn.shape[0]
    d_ff = wg.shape[1]
    grid = (t // tm, d_ff // tf)
    return pl.pallas_call(
        _ffn_kernel,
        out_shape=jax.ShapeDtypeStruct((t, D_MODEL), F32),
        grid=grid,
        in_specs=[
            pl.BlockSpec((tm, D_MODEL), lambda i, j: (i, 0)),
            pl.BlockSpec((tm, D_MODEL), lambda i, j: (i, 0)),
            pl.BlockSpec((D_MODEL, tf), lambda i, j: (0, j)),
            pl.BlockSpec((D_MODEL, tf), lambda i, j: (0, j)),
            pl.BlockSpec((tf, D_MODEL), lambda i, j: (j, 0)),
        ],
        out_specs=pl.BlockSpec((tm, D_MODEL), lambda i, j: (i, 0)),
        compiler_params=pltpu.CompilerParams(
            dimension_semantics=("arbitrary", "arbitrary"),
            vmem_limit_bytes=VMEM_LIMIT),
        name="ffn",
    )(hn, h, wg, wu, wd)


def kernel(x, norm1_g, w_in, q_norm_g, k_norm_g, lambda_q1, lambda_k1, lambda_q2, lambda_k2,
           subln_g, w_branch_a, w_branch_b, w_out, norm2_g, w_ffn_gate, w_ffn_up, w_ffn_down):
    b, s, d = x.shape
    t = b * s
    layer = 0
    x2d = x.reshape(t, d)
    lam_params = jnp.stack([lambda_q1[layer], lambda_k1[layer], lambda_q2[layer], lambda_k2[layer]])

    w_in_bf = w_in[layer].astype(BF16)
    proj, xn = _in_proj(x2d, norm1_g[layer].reshape(1, d), w_in_bf,
                        q_norm_g[layer].reshape(1, HEAD_DIM), k_norm_g[layer].reshape(1, HEAD_DIM))
    proj3d = proj.reshape(b, s, OFF_GA)

    out_a, gates = _sb_attn(proj3d, xn.reshape(b, s, d), w_in_bf)
    out_a = out_a.reshape(t, SB_WIDTH)
    out_b, wg, wu, wd = _df_attn(proj3d, lam_params, subln_g[layer].reshape(1, DF_V_DIM),
                                 w_ffn_gate[layer], w_ffn_up[layer], w_ffn_down[layer])
    out_b = out_b.reshape(t, DF_V_WIDTH)

    merged = _merge(out_a, w_branch_a[layer].astype(BF16), out_b, w_branch_b[layer].astype(BF16),
                    gates.reshape(t, 2 * d))
    h, hn = _out_proj(merged, w_out[layer].astype(BF16), x2d, norm2_g[layer].reshape(1, d))
    out = _ffn(hn, h, wg, wu, wd)
    return out.reshape(b, s, d)
```

```python
import math

import numpy as np
import jax
import jax.numpy as jnp
from jax import lax
from jax.experimental import pallas as pl
from jax.experimental.pallas import tpu as pltpu

F32 = jnp.float32
BF16 = jnp.bfloat16

D_MODEL = 2048
SB_HEADS = 8
HEAD_DIM = 128
DF_HEADS = 4
DF_V_DIM = 256
CHUNK = 64
SB_WIDTH = SB_HEADS * HEAD_DIM
DF_QK_WIDTH = DF_HEADS * 2 * HEAD_DIM
DF_V_WIDTH = DF_HEADS * DF_V_DIM
IN_WIDTH = 3 * SB_WIDTH + 2 * DF_QK_WIDTH + DF_V_WIDTH + 2 * D_MODEL
EPS = 1e-6
SUBLN_EPS = 1e-5
LAMBDA_INIT = 0.8 - 0.6 * math.exp(-0.3 * 0)
LOG2E = math.log2(math.e)
Q_SCALE = LOG2E / math.sqrt(HEAD_DIM)

OFF_SBQ = 0
OFF_SBK = SB_WIDTH
OFF_SBV = 2 * SB_WIDTH
OFF_DFQ = 3 * SB_WIDTH
OFF_DFK = OFF_DFQ + DF_QK_WIDTH
OFF_DFV = OFF_DFK + DF_QK_WIDTH
OFF_GA = OFF_DFV + DF_V_WIDTH
OFF_GB = OFF_GA + D_MODEL

VMEM_LIMIT = 56 * 1024 * 1024
NEG_BIG = -1e30


def _dot(a, b):
    return jnp.dot(a, b, preferred_element_type=F32)


def _dot_nt(a, b):
    return lax.dot_general(a, b, (((1,), (1,)), ((), ())), preferred_element_type=F32)


PROJ_TN = 1024


def _in_proj_kernel(x_ref, g_ref, w_ref, qg_ref, kg_ref, o_ref, xn_ref):
    j = pl.program_id(1)

    @pl.when(j == 0)
    def _():
        x = x_ref[...]
        ms = jnp.mean(x * x, axis=-1, keepdims=True)
        xn_ref[...] = (x * lax.rsqrt(ms + EPS) * g_ref[...]).astype(BF16)

    is_sbq = j == OFF_SBQ // PROJ_TN
    is_q = j == OFF_DFQ // PROJ_TN
    is_qk = is_q | (j == OFF_DFK // PROJ_TN)

    def proj():
        return _dot(xn_ref[...], w_ref[...])

    @pl.when(jnp.logical_not(is_qk))
    def _():
        o_ref[...] = (proj() * jnp.where(is_sbq, Q_SCALE, 1.0)).astype(o_ref.dtype)

    @pl.when(is_qk)
    def _():
        acc = proj()
        gain = jnp.where(is_q, qg_ref[...] * Q_SCALE, kg_ref[...])
        for c in range(PROJ_TN // HEAD_DIM):
            a = acc[:, c * HEAD_DIM:(c + 1) * HEAD_DIM]
            ms = jnp.mean(a * a, axis=-1, keepdims=True)
            o_ref[:, c * HEAD_DIM:(c + 1) * HEAD_DIM] = (
                a * lax.rsqrt(ms + EPS) * gain).astype(o_ref.dtype)


def _in_proj(x2d, g1, w_in_bf, qg, kg, tm=1024):
    t = x2d.shape[0]
    grid = (t // tm, OFF_GA // PROJ_TN)
    return pl.pallas_call(
        _in_proj_kernel,
        out_shape=(jax.ShapeDtypeStruct((t, OFF_GA), BF16),
                   jax.ShapeDtypeStruct((t, D_MODEL), BF16)),
        grid=grid,
        in_specs=[
            pl.BlockSpec((tm, D_MODEL), lambda i, j: (i, 0)),
            pl.BlockSpec((1, D_MODEL), lambda i, j: (0, 0)),
            pl.BlockSpec((D_MODEL, PROJ_TN), lambda i, j: (0, j)),
            pl.BlockSpec((1, HEAD_DIM), lambda i, j: (0, 0)),
            pl.BlockSpec((1, HEAD_DIM), lambda i, j: (0, 0)),
        ],
        out_specs=(pl.BlockSpec((tm, PROJ_TN), lambda i, j: (i, j)),
                   pl.BlockSpec((tm, D_MODEL), lambda i, j: (i, 0))),
        compiler_params=pltpu.CompilerParams(
            dimension_semantics=("arbitrary", "arbitrary"),
            vmem_limit_bytes=VMEM_LIMIT),
        name="in_proj",
    )(x2d, g1, w_in_bf, qg, kg)


SB_T = 256
SB_G = 4


def _softplus2(a):
    return jnp.maximum(a, 0.0) + jnp.log2(1.0 + jnp.exp2(jnp.minimum(a, -a)))


def _split_hi_lo(a):
    hi = a.astype(BF16)
    lo = (a - hi.astype(F32)).astype(BF16)
    return hi, lo


def _eye(n):
    return (lax.broadcasted_iota(jnp.int32, (n, n), 0)
            == lax.broadcasted_iota(jnp.int32, (n, n), 1)).astype(BF16)


def _fill_vt(v_ref, vt_ref, t, head_dim=HEAD_DIM, head_rows=HEAD_DIM):
    eye = _eye(HEAD_DIM)
    n_heads = v_ref.shape[2] // head_dim

    def body(c, _):
        start = pl.multiple_of(c * t, t)
        for h in range(n_heads):
            for g in range(head_dim // HEAD_DIM):
                lanes = slice(h * head_dim + g * HEAD_DIM, h * head_dim + (g + 1) * HEAD_DIM)
                rows = slice(h * head_rows + g * HEAD_DIM, h * head_rows + (g + 1) * HEAD_DIM)
                vt_ref[c, rows, :] = _dot_nt(eye, v_ref[0, pl.ds(start, t), lanes]).astype(BF16)
            if head_rows > head_dim:
                vt_ref[c, h * head_rows + head_dim:(h + 1) * head_rows, :] = jnp.ones(
                    (head_rows - head_dim, t), BF16)
        return 0

    lax.fori_loop(0, v_ref.shape[1] // t, body, 0)


SB_EXIT_LOG2 = 160.0
NORM_SLACK = 1.01


def _max_key_norm2(k_ref, kn_ref, t):
    ones8 = jnp.ones((8, HEAD_DIM), BF16)
    n_groups = k_ref.shape[2] // HEAD_DIM

    def body(c, mx):
        start = pl.multiple_of(c * t, t)
        out = []
        for g in range(n_groups):
            kf = k_ref[0, pl.ds(start, t), g * HEAD_DIM:(g + 1) * HEAD_DIM].astype(F32)
            out.append(jnp.maximum(mx[g], _dot_nt(ones8, (kf * kf).astype(BF16))))
        return tuple(out)

    mx = lax.fori_loop(0, k_ref.shape[1] // t, body,
                       tuple(jnp.zeros((8, t), F32) for _ in range(n_groups)))
    for g in range(n_groups):
        kn_ref[g:g + 1, :] = jnp.broadcast_to(
            jnp.max(mx[g][0:1, :], axis=1, keepdims=True), (1, HEAD_DIM))


GATE_CHUNKS = 8
GATE_AFTER_QK = 6
GATE_AFTER_CUM = 2


WD_CAST_EVERY = 4


def _sb_attn_kernel(q_ref, k_ref, v_ref, xn_ref, wgate_ref, wd_ref, o_ref, gate_ref, wd_out_ref,
                    vt_ref, a_ref, hl_ref, e_ref, kn_ref, gbuf_ref):
    qi = pl.program_id(2)
    t = SB_T
    heads = range(SB_G)

    @pl.when(qi % WD_CAST_EVERY == 0)
    def _():
        wd_out_ref[...] = wd_ref[...].astype(BF16)

    @pl.when(qi == 0)
    def _():
        _fill_vt(v_ref, vt_ref, t)
        _max_key_norm2(k_ref, kn_ref, t)

    r2 = lax.broadcasted_iota(jnp.int32, (t, 2 * t), 0)
    c2 = lax.broadcasted_iota(jnp.int32, (t, 2 * t), 1)
    l2 = ((c2 & (t - 1)) >= r2).astype(BF16)
    lanes = [slice(g * HEAD_DIM, (g + 1) * HEAD_DIM) for g in heads]

    def block_start(j):
        return pl.multiple_of(jnp.maximum(qi - j, 0) * t, t)

    def strict_mask():
        key = lax.broadcasted_iota(jnp.int32, (t, t), 0)
        qry = lax.broadcasted_iota(jnp.int32, (t, t), 1)
        return key < qry

    def s1_matmul(j):
        start = block_start(j)
        return [_dot_nt(k_ref[0, pl.ds(start, t), lanes[g]], q_ref[0, :, lanes[g]]) for g in heads]

    def s1_finish(a, slot, masked):
        for g in heads:
            sp = _softplus2(a[g])
            if masked:
                sp = jnp.where(strict_mask(), sp, 0.0)
            hi, lo = _split_hi_lo(sp)
            a_ref[slot, g] = a[g]
            hl_ref[slot, g, 0:t, :] = hi
            hl_ref[slot, g, t:2 * t, :] = lo

    def s2_matmul(slot):
        return [_dot(l2, hl_ref[slot, g]) for g in heads]

    def s2_finish(cum, slot, masked):
        for g in heads:
            e0 = a_ref[slot, g] - cum[g]
            if masked:
                e0 = jnp.where(strict_mask(), e0, NEG_BIG)
            e_ref[slot, g] = e0
        return tuple(c[0:1, :] for c in cum)

    def s3_weights(slot, carry):
        return [jnp.exp2(e_ref[slot, g] - carry[g]).astype(BF16) for g in heads]

    def s3_matmul(j, w):
        kb = jnp.maximum(qi - j, 0)
        return [_dot(vt_ref[kb, lanes[g], :], w[g]) for g in heads]

    def step(i, slot, st):
        acc, carry, sums = st
        w = s3_weights(slot, carry)
        a = s1_matmul(i)
        cum = s2_matmul(1 - slot)
        pv = s3_matmul(i - 2, w)
        s1_finish(a, slot, False)
        new_sums = s2_finish(cum, 1 - slot, False)
        return (tuple(acc[g] + pv[g] for g in heads),
                tuple(carry[g] + sums[g] for g in heads), new_sums)

    ones8 = jnp.ones((8, HEAD_DIM), BF16)
    bound = []
    for g in heads:
        qf = q_ref[0, :, lanes[g]].astype(F32)
        qn2 = _dot_nt(ones8, (qf * qf).astype(BF16))[0:1, :]
        bound.append(jnp.sqrt(qn2 * kn_ref[g:g + 1, 0:1]) * NORM_SLACK)

    def all_underflow(carry):
        slack = bound[0] - carry[0]
        for g in heads[1:]:
            slack = jnp.maximum(slack, bound[g] - carry[g])
        return jnp.max(slack) < -SB_EXIT_LOG2

    gate_cols = wgate_ref.shape[1] // GATE_CHUNKS

    def gate_chunk(c):
        cols = slice(c * gate_cols, (c + 1) * gate_cols)
        gbuf_ref[:, cols] = _dot(xn_ref[0], wgate_ref[:, cols])

    a0 = s1_matmul(0)
    a1 = s1_matmul(1)
    for c in range(0, GATE_AFTER_QK):
        gate_chunk(c)
    s1_finish(a0, 0, True)
    s1_finish(a1, 1, False)
    cum0 = s2_matmul(0)
    cum1 = s2_matmul(1)
    for c in range(GATE_AFTER_QK, GATE_AFTER_QK + GATE_AFTER_CUM):
        gate_chunk(c)
    sums0 = s2_finish(cum0, 0, True)
    sums1 = s2_finish(cum1, 1, False)
    w0 = s3_weights(0, tuple(jnp.zeros((1, t), F32) for _ in heads))
    w1 = s3_weights(1, sums0)
    pv0 = s3_matmul(0, w0)
    pv1 = s3_matmul(1, w1)
    for c in range(GATE_AFTER_QK + GATE_AFTER_CUM, GATE_CHUNKS):
        gate_chunk(c)
    gate_ref[0] = gbuf_ref[...].astype(gate_ref.dtype)
    acc2 = tuple(pv0[g] + jnp.where(qi > 0, pv1[g], 0.0) for g in heads)
    carry2 = tuple(sums0[g] + sums1[g] for g in heads)

    def remaining_blocks(acc, carry):
        s1_finish(s1_matmul(2), 0, False)
        a = s1_matmul(3)
        cum = s2_matmul(0)
        s1_finish(a, 1, False)
        sums = s2_finish(cum, 0, False)
        pairs = (qi - 2) // 2

        def cond(st):
            k, done = st[0], st[1]
            return (k < pairs) & jnp.logical_not(done)

        def body(st):
            k = st[0]
            acc, carry, sums = step(2 * k + 5, 1, step(2 * k + 4, 0, st[2:]))
            return (k + 1, all_underflow(carry), acc, carry, sums)

        k, _, acc, carry, sums = lax.while_loop(
            cond, body, (jnp.int32(0), False, acc, carry, sums))

        w = s3_weights(0, carry)
        cum = s2_matmul(1)
        pv = s3_matmul(2 * k + 2, w)
        s2_finish(cum, 1, False)
        acc = tuple(acc[g] + pv[g] for g in heads)
        carry = tuple(carry[g] + sums[g] for g in heads)
        pv = s3_matmul(2 * k + 3, s3_weights(1, carry))
        last_is_real = 2 * k + 3 <= qi
        return tuple(acc[g] + jnp.where(last_is_real, pv[g], 0.0) for g in heads)

    more = (qi >= 2) & jnp.logical_not(all_underflow(carry2))
    acc = lax.cond(more, lambda: remaining_blocks(acc2, carry2), lambda: acc2)
    for g in heads:
        o_ref[0, :, lanes[g]] = acc[g].T.astype(o_ref.dtype)


def _sb_attn(proj3d, xn3d, w_in_bf, wd):
    b, s, _ = proj3d.shape
    t = SB_T
    w = SB_G * HEAD_DIM
    groups = SB_HEADS // SB_G
    nq = s // t
    grid = (b, groups, nq)
    cq, ck, cv = OFF_SBQ // w, OFF_SBK // w, OFF_SBV // w
    gate_w = 2 * D_MODEL // groups
    cg = OFF_GA // gate_w
    single = pl.Buffered(1)
    slices = b * groups * nq // WD_CAST_EVERY
    wd3 = wd.reshape(slices, wd.shape[0] // slices, wd.shape[1])
    wd_spec = pl.BlockSpec(
        (1,) + wd3.shape[1:],
        lambda bi, h, qi: (((bi * groups + h) * nq + qi) // WD_CAST_EVERY, 0, 0))
    out_a, gates, wd_bf = pl.pallas_call(
        _sb_attn_kernel,
        out_shape=(jax.ShapeDtypeStruct((b, s, SB_WIDTH), BF16),
                   jax.ShapeDtypeStruct((b, s, 2 * D_MODEL), BF16),
                   jax.ShapeDtypeStruct(wd3.shape, BF16)),
        grid=grid,
        in_specs=[
            pl.BlockSpec((1, t, w), lambda bi, h, qi: (bi, qi, cq + h)),
            pl.BlockSpec((1, s, w), lambda bi, h, qi: (bi, 0, ck + h)),
            pl.BlockSpec((1, s, w), lambda bi, h, qi: (bi, 0, cv + h)),
            pl.BlockSpec((1, t, D_MODEL), lambda bi, h, qi: (bi, qi, 0)),
            pl.BlockSpec((D_MODEL, gate_w), lambda bi, h, qi: (0, cg + h), pipeline_mode=single),
            wd_spec,
        ],
        out_specs=(pl.BlockSpec((1, t, w), lambda bi, h, qi: (bi, qi, h)),
                   pl.BlockSpec((1, t, gate_w), lambda bi, h, qi: (bi, qi, h)),
                   wd_spec),
        scratch_shapes=[pltpu.VMEM((s // t, w, t), BF16),
                        pltpu.VMEM((2, SB_G, t, t), F32),
                        pltpu.VMEM((2, SB_G, 2 * t, t), BF16),
                        pltpu.VMEM((2, SB_G, t, t), F32),
                        pltpu.VMEM((SB_G, HEAD_DIM), F32),
                        pltpu.VMEM((t, gate_w), F32)],
        compiler_params=pltpu.CompilerParams(
            dimension_semantics=("arbitrary", "arbitrary", "arbitrary"),
            vmem_limit_bytes=VMEM_LIMIT),
        name="sb_attn",
    )(proj3d, proj3d, proj3d, xn3d, w_in_bf, wd3)
    return out_a, gates, wd_bf.reshape(wd.shape)


DF_T = 256
DF_SLOPES = [2.0 ** (-8.0 * (h + 1) / DF_HEADS) for h in range(DF_HEADS)]
DF_V_AUG = DF_V_DIM + 16
DF_SLOPE_PIECES = 4


def _bf16_pieces(x, n):
    pieces = []
    for _ in range(n):
        p = float(np.asarray(x, dtype=np.float32).astype(jnp.bfloat16).astype(np.float32))
        pieces.append(p)
        x = x - p
    return pieces


def _df_attn_kernel(lamp_ref, q_ref, k_ref, v_ref, sg_ref, wg_ref, wu_ref,
                    o_ref, wg_out_ref, wu_out_ref,
                    bias_ref, kc_ref, qc_ref, acc_ref, vt_ref, p_ref, al_ref, m_ref):
    bi = pl.program_id(0)
    qs = pl.program_id(1)
    t = DF_T
    va = DF_V_AUG

    wg_out_ref[...] = wg_ref[...].astype(BF16)
    wu_out_ref[...] = wu_ref[...].astype(BF16)

    @pl.when(qs == 0)
    def _():
        _fill_vt(v_ref, vt_ref, t, DF_V_DIM, va)

    @pl.when((bi == 0) & (qs == 0))
    def _():
        key = lax.broadcasted_iota(jnp.int32, (t, t), 0)
        qry = lax.broadcasted_iota(jnp.int32, (t, t), 1)
        rel = (qry - key).astype(F32)
        allowed = (key // CHUNK) <= (qry // CHUNK)
        pos = lax.broadcasted_iota(jnp.int32, (t, HEAD_DIM), 0).astype(F32)
        col = lax.broadcasted_iota(jnp.int32, (t, HEAD_DIM), 1)
        n = DF_SLOPE_PIECES
        for h in range(DF_HEADS):
            sl = DF_SLOPES[h] * LOG2E
            bias_ref[h] = jnp.where(allowed, -sl * jnp.abs(rel), NEG_BIG)
            pieces = jnp.zeros((t, HEAD_DIM), F32)
            for idx, piece in enumerate(_bf16_pieces(sl, n)):
                pieces = jnp.where((col == idx) | (col == n + idx), piece, pieces)
            kc_ref[h] = jnp.where(col < n, pos, jnp.where(col < 2 * n, pieces, 0.0)).astype(BF16)
            qc_ref[h] = jnp.where(col < n, pieces, jnp.where(col < 2 * n, -pos, 0.0)).astype(BF16)

    all_chains = [(u, h, c) for u in range(2) for h in range(DF_HEADS) for c in range(2)]
    u1_chains = [ch for ch in all_chains if ch[0] == 1]

    def chain_id(ch):
        u, h, c = ch
        return (u * DF_HEADS + h) * 2 + c

    def lanes(h, c):
        return slice((2 * h + c) * HEAD_DIM, (2 * h + c + 1) * HEAD_DIM)

    def q_blk(u, h, c):
        return q_ref[0, u * t:(u + 1) * t, lanes(h, c)]

    def k_blk(kb, h, c):
        return k_ref[0, pl.ds(pl.multiple_of(kb * t, t), t), lanes(h, c)]

    def s1_offdiag(kb, chains):
        return [_dot_nt(jnp.concatenate([k_blk(kb, h, c), kc_ref[h]], axis=1),
                        jnp.concatenate([q_blk(u, h, c), qc_ref[h]], axis=1))
                for u, h, c in chains]

    def s1_diag():
        return [_dot_nt(k_blk(2 * qs + u, h, c), q_blk(u, h, c)) + bias_ref[h]
                for u, h, c in all_chains]

    def s1_finish(s, chains, blocks_away, slot):
        for s_i, ch in zip(s, chains):
            i = chain_id(ch)
            row = slice(i, i + 1)
            shift = -(DF_SLOPES[ch[1]] * LOG2E) * (blocks_away(ch[0]) * t).astype(F32)
            m = m_ref[row, :]
            m_new = jnp.maximum(m, jnp.max(s_i, axis=0, keepdims=True) + shift)
            m_ref[row, :] = m_new
            al_ref[slot, row, :] = jnp.exp2(m - m_new)
            p_ref[slot, i] = jnp.exp2(s_i - (m_new - shift)).astype(BF16)

    def s2_matmul(chains, key_block, slot):
        return [_dot(vt_ref[key_block(u), h * va:(h + 1) * va, :], p_ref[slot, chain_id((u, h, c))])
                for u, h, c in chains]

    def s2_finish(pv, chains, slot):
        for pv_i, ch in zip(pv, chains):
            i = chain_id(ch)
            u, h, c = ch
            acc_ref[u, h, c] = al_ref[slot, i:i + 1, :] * acc_ref[u, h, c] + pv_i

    acc_ref[...] = jnp.zeros_like(acc_ref)
    m_ref[...] = jnp.full_like(m_ref, NEG_BIG)

    s1_finish(s1_offdiag(2 * qs, u1_chains), u1_chains, lambda u: jnp.int32(1), 0)

    s = s1_diag()
    pv = s2_matmul(u1_chains, lambda u: 2 * qs, 0)
    s2_finish(pv, u1_chains, 0)
    s1_finish(s, all_chains, lambda u: jnp.int32(0), 1)

    def common_step(kb, slot, prev_block):
        s = s1_offdiag(kb, all_chains)
        pv = s2_matmul(all_chains, prev_block, 1 - slot)
        s2_finish(pv, all_chains, 1 - slot)
        s1_finish(s, all_chains, lambda u: 2 * qs + u - kb, slot)

    def body(k, carry):
        common_step(2 * k, 0, lambda u: jnp.where(k == 0, 2 * qs + u, 2 * k - 1))
        common_step(2 * k + 1, 1, lambda u: 2 * k)
        return carry

    lax.fori_loop(0, qs, body, 0)

    pv = s2_matmul(all_chains, lambda u: jnp.where(qs == 0, 2 * qs + u, 2 * qs - 1), 1)
    s2_finish(pv, all_chains, 1)

    lp = lamp_ref[...]
    lam = (jnp.exp(jnp.sum(lp[0:1] * lp[1:2], axis=-1, keepdims=True))
           - jnp.exp(jnp.sum(lp[2:3] * lp[3:4], axis=-1, keepdims=True)) + LAMBDA_INIT)
    for u in range(2):
        for h in range(DF_HEADS):
            a1, a2 = acc_ref[u, h, 0], acc_ref[u, h, 1]
            l1, l2 = a1[DF_V_DIM:DF_V_DIM + 1, :], a2[DF_V_DIM:DF_V_DIM + 1, :]
            out_t = a1[0:DF_V_DIM, :] / l1 - lam * (a2[0:DF_V_DIM, :] / l2)
            ms = jnp.mean(out_t * out_t, axis=0, keepdims=True)
            out = (out_t * lax.rsqrt(ms + SUBLN_EPS)).T * (sg_ref[...] * (1.0 - LAMBDA_INIT))
            o_ref[0, u * t:(u + 1) * t, h * DF_V_DIM:(h + 1) * DF_V_DIM] = out.astype(o_ref.dtype)


def _df_attn(proj3d, lam_params, subln_g, wg, wu):
    b, s, _ = proj3d.shape
    t = DF_T
    tq = 2 * t
    nq = s // tq
    grid = (b, nq)
    steps = b * nq
    cq, ck, cv = OFF_DFQ // DF_QK_WIDTH, OFF_DFK // DF_QK_WIDTH, OFF_DFV // DF_V_WIDTH
    single = pl.Buffered(1)
    d, d_ff = wg.shape
    wg3 = wg.reshape(steps, d // steps, d_ff)
    wu3 = wu.reshape(steps, d // steps, d_ff)
    gu_spec = pl.BlockSpec((1,) + wg3.shape[1:], lambda bi, qs: (bi * nq + qs, 0, 0))
    out_b, wg_bf, wu_bf = pl.pallas_call(
        _df_attn_kernel,
        out_shape=(jax.ShapeDtypeStruct((b, s, DF_V_WIDTH), BF16),
                   jax.ShapeDtypeStruct(wg3.shape, BF16),
                   jax.ShapeDtypeStruct(wu3.shape, BF16)),
        grid=grid,
        in_specs=[
            pl.BlockSpec((4, HEAD_DIM), lambda bi, qs: (0, 0)),
            pl.BlockSpec((1, tq, DF_QK_WIDTH), lambda bi, qs: (bi, qs, cq)),
            pl.BlockSpec((1, s, DF_QK_WIDTH), lambda bi, qs: (bi, 0, ck), pipeline_mode=single),
            pl.BlockSpec((1, s, DF_V_WIDTH), lambda bi, qs: (bi, 0, cv), pipeline_mode=single),
            pl.BlockSpec((1, DF_V_DIM), lambda bi, qs: (0, 0)),
            gu_spec, gu_spec,
        ],
        out_specs=(pl.BlockSpec((1, tq, DF_V_WIDTH), lambda bi, qs: (bi, qs, 0)),
                   gu_spec, gu_spec),
        scratch_shapes=[pltpu.VMEM((DF_HEADS, t, t), F32),
                        pltpu.VMEM((DF_HEADS, t, HEAD_DIM), BF16),
                        pltpu.VMEM((DF_HEADS, t, HEAD_DIM), BF16),
                        pltpu.VMEM((2, DF_HEADS, 2, DF_V_AUG, t), F32),
                        pltpu.VMEM((s // t, DF_HEADS * DF_V_AUG, t), BF16),
                        pltpu.VMEM((2, 4 * DF_HEADS, t, t), BF16),
                        pltpu.VMEM((2, 4 * DF_HEADS, t), F32),
                        pltpu.VMEM((4 * DF_HEADS, t), F32)],
        compiler_params=pltpu.CompilerParams(
            dimension_semantics=("arbitrary", "arbitrary"),
            vmem_limit_bytes=VMEM_LIMIT),
        name="df_attn",
    )(lam_params, proj3d, proj3d, proj3d, subln_g, wg3, wu3)
    return out_b, wg_bf.reshape(wg.shape), wu_bf.reshape(wu.shape)


def _merge_kernel(a_ref, wa_ref, b_ref, wb_ref, ga_ref, gb_ref, o_ref):
    pa = _dot(a_ref[...], wa_ref[...])
    pb = _dot(b_ref[...], wb_ref[...])
    ga = jax.nn.sigmoid(ga_ref[...].astype(F32))
    gb = jax.nn.sigmoid(gb_ref[...].astype(F32))
    o_ref[...] = (ga * pa + gb * pb).astype(o_ref.dtype)


def _merge(out_a, wa, out_b, wb, gates, tm=1024, tn=1024):
    t = out_a.shape[0]
    grid = (t // tm, D_MODEL // tn)
    ca, cb = 0, D_MODEL // tn
    return pl.pallas_call(
        _merge_kernel,
        out_shape=jax.ShapeDtypeStruct((t, D_MODEL), BF16),
        grid=grid,
        in_specs=[
            pl.BlockSpec((tm, SB_WIDTH), lambda i, j: (i, 0)),
            pl.BlockSpec((SB_WIDTH, tn), lambda i, j: (0, j)),
            pl.BlockSpec((tm, DF_V_WIDTH), lambda i, j: (i, 0)),
            pl.BlockSpec((DF_V_WIDTH, tn), lambda i, j: (0, j)),
            pl.BlockSpec((tm, tn), lambda i, j: (i, ca + j)),
            pl.BlockSpec((tm, tn), lambda i, j: (i, cb + j)),
        ],
        out_specs=pl.BlockSpec((tm, tn), lambda i, j: (i, j)),
        compiler_params=pltpu.CompilerParams(
            dimension_semantics=("arbitrary", "arbitrary"),
            vmem_limit_bytes=VMEM_LIMIT),
        name="merge",
    )(out_a, wa, out_b, wb, gates, gates)


def _out_proj_kernel(m_ref, w_ref, x_ref, g_ref, h_ref, hn_ref):
    h = x_ref[...] + _dot(m_ref[...], w_ref[...])
    h_ref[...] = h
    ms = jnp.mean(h * h, axis=-1, keepdims=True)
    hn_ref[...] = (h * lax.rsqrt(ms + EPS) * g_ref[...]).astype(hn_ref.dtype)


def _out_proj(merged, w_out, x2d, g2, tm=512):
    t = merged.shape[0]
    grid = (t // tm,)
    return pl.pallas_call(
        _out_proj_kernel,
        out_shape=(jax.ShapeDtypeStruct((t, D_MODEL), F32),
                   jax.ShapeDtypeStruct((t, D_MODEL), BF16)),
        grid=grid,
        in_specs=[
            pl.BlockSpec((tm, D_MODEL), lambda i: (i, 0)),
            pl.BlockSpec((D_MODEL, D_MODEL), lambda i: (0, 0)),
            pl.BlockSpec((tm, D_MODEL), lambda i: (i, 0)),
            pl.BlockSpec((1, D_MODEL), lambda i: (0, 0)),
        ],
        out_specs=(pl.BlockSpec((tm, D_MODEL), lambda i: (i, 0)),
                   pl.BlockSpec((tm, D_MODEL), lambda i: (i, 0))),
        compiler_params=pltpu.CompilerParams(
            dimension_semantics=("arbitrary",),
            vmem_limit_bytes=VMEM_LIMIT),
        name="out_proj",
    )(merged, w_out, x2d, g2)


FFN_SUB = 2


def _ffn_kernel(hn_ref, h_ref, wg_ref, wu_ref, wd_ref, o_ref):
    j = pl.program_id(1)

    @pl.when(j == 0)
    def _():
        o_ref[...] = h_ref[...]

    hn = hn_ref[...]
    tf = wg_ref.shape[1]
    sub = tf // FFN_SUB
    gu = []
    for c in range(FFN_SUB):
        cols = slice(c * sub, (c + 1) * sub)
        gu.append((_dot(hn, wg_ref[:, cols]), _dot(hn, wu_ref[:, cols])))
    acc = None
    for c in range(FFN_SUB):
        g, u = gu[c]
        hid = (g * jax.nn.sigmoid(g) * u).astype(BF16)
        d = _dot(hid, wd_ref[c * sub:(c + 1) * sub, :])
        acc = d if acc is None else acc + d
    o_ref[...] += acc


def _ffn(hn, h, wg, wu, wd, tm=512, tf=512):
    t = hn.shape[0]
    d_ff = wg.shape[1]
    grid = (t // tm, d_ff // tf)
    return pl.pallas_call(
        _ffn_kernel,
        out_shape=jax.ShapeDtypeStruct((t, D_MODEL), F32),
        grid=grid,
        in_specs=[
            pl.BlockSpec((tm, D_MODEL), lambda i, j: (i, 0)),
            pl.BlockSpec((tm, D_MODEL), lambda i, j: (i, 0)),
            pl.BlockSpec((D_MODEL, tf), lambda i, j: (0, j)),
            pl.BlockSpec((D_MODEL, tf), lambda i, j: (0, j)),
            pl.BlockSpec((tf, D_MODEL), lambda i, j: (j, 0)),
        ],
        out_specs=pl.BlockSpec((tm, D_MODEL), lambda i, j: (i, 0)),
        compiler_params=pltpu.CompilerParams(
            dimension_semantics=("arbitrary", "arbitrary"),
            vmem_limit_bytes=VMEM_LIMIT),
        name="ffn",
    )(hn, h, wg, wu, wd)


def kernel(x, norm1_g, w_in, q_norm_g, k_norm_g, lambda_q1, lambda_k1, lambda_q2, lambda_k2,
           subln_g, w_branch_a, w_branch_b, w_out, norm2_g, w_ffn_gate, w_ffn_up, w_ffn_down):
    b, s, d = x.shape
    t = b * s
    layer = 0
    x2d = x.reshape(t, d)
    lam_params = jnp.stack([lambda_q1[layer], lambda_k1[layer], lambda_q2[layer], lambda_k2[layer]])

    w_in_bf = w_in[layer].astype(BF16)
    proj, xn = _in_proj(x2d, norm1_g[layer].reshape(1, d), w_in_bf,
                        q_norm_g[layer].reshape(1, HEAD_DIM), k_norm_g[layer].reshape(1, HEAD_DIM))
    proj3d = proj.reshape(b, s, OFF_GA)

    out_a, gates, wd = _sb_attn(proj3d, xn.reshape(b, s, d), w_in_bf, w_ffn_down[layer])
    out_a = out_a.reshape(t, SB_WIDTH)
    out_b, wg, wu = _df_attn(proj3d, lam_params, subln_g[layer].reshape(1, DF_V_DIM),
                             w_ffn_gate[layer], w_ffn_up[layer])
    out_b = out_b.reshape(t, DF_V_WIDTH)

    merged = _merge(out_a, w_branch_a[layer].astype(BF16), out_b, w_branch_b[layer].astype(BF16),
                    gates.reshape(t, 2 * d))
    h, hn = _out_proj(merged, w_out[layer].astype(BF16), x2d, norm2_g[layer].reshape(1, d))
    out = _ffn(hn, h, wg, wu, wd)
    return out.reshape(b, s, d)
```

```python
import math

import numpy as np
import jax
import jax.numpy as jnp
from jax import lax
from jax.experimental import pallas as pl
from jax.experimental.pallas import tpu as pltpu

F32 = jnp.float32
BF16 = jnp.bfloat16

D_MODEL = 2048
SB_HEADS = 8
HEAD_DIM = 128
DF_HEADS = 4
DF_V_DIM = 256
CHUNK = 64
SB_WIDTH = SB_HEADS * HEAD_DIM
DF_QK_WIDTH = DF_HEADS * 2 * HEAD_DIM
DF_V_WIDTH = DF_HEADS * DF_V_DIM
IN_WIDTH = 3 * SB_WIDTH + 2 * DF_QK_WIDTH + DF_V_WIDTH + 2 * D_MODEL
EPS = 1e-6
SUBLN_EPS = 1e-5
LAMBDA_INIT = 0.8 - 0.6 * math.exp(-0.3 * 0)
LOG2E = math.log2(math.e)
Q_SCALE = LOG2E / math.sqrt(HEAD_DIM)

OFF_SBQ = 0
OFF_SBK = SB_WIDTH
OFF_SBV = 2 * SB_WIDTH
OFF_DFQ = 3 * SB_WIDTH
OFF_DFK = OFF_DFQ + DF_QK_WIDTH
OFF_DFV = OFF_DFK + DF_QK_WIDTH
OFF_GA = OFF_DFV + DF_V_WIDTH
OFF_GB = OFF_GA + D_MODEL

VMEM_LIMIT = 56 * 1024 * 1024
NEG_BIG = -1e30


def _dot(a, b):
    return jnp.dot(a, b, preferred_element_type=F32)


def _dot_nt(a, b):
    return lax.dot_general(a, b, (((1,), (1,)), ((), ())), preferred_element_type=F32)


PROJ_TN = 1024


def _in_proj_kernel(x_ref, g_ref, w_ref, qg_ref, kg_ref, o_ref, xn_ref):
    j = pl.program_id(1)

    @pl.when(j == 0)
    def _():
        x = x_ref[...]
        ms = jnp.mean(x * x, axis=-1, keepdims=True)
        xn_ref[...] = (x * lax.rsqrt(ms + EPS) * g_ref[...]).astype(BF16)

    is_sbq = j == OFF_SBQ // PROJ_TN
    is_q = j == OFF_DFQ // PROJ_TN
    is_qk = is_q | (j == OFF_DFK // PROJ_TN)

    def proj():
        return _dot(xn_ref[...], w_ref[...])

    @pl.when(jnp.logical_not(is_qk))
    def _():
        o_ref[...] = (proj() * jnp.where(is_sbq, Q_SCALE, 1.0)).astype(o_ref.dtype)

    @pl.when(is_qk)
    def _():
        acc = proj()
        gain = jnp.where(is_q, qg_ref[...] * Q_SCALE, kg_ref[...])
        for c in range(PROJ_TN // HEAD_DIM):
            a = acc[:, c * HEAD_DIM:(c + 1) * HEAD_DIM]
            ms = jnp.mean(a * a, axis=-1, keepdims=True)
            o_ref[:, c * HEAD_DIM:(c + 1) * HEAD_DIM] = (
                a * lax.rsqrt(ms + EPS) * gain).astype(o_ref.dtype)


def _in_proj(x2d, g1, w_in_bf, qg, kg, tm=1024):
    t = x2d.shape[0]
    grid = (t // tm, OFF_GA // PROJ_TN)
    return pl.pallas_call(
        _in_proj_kernel,
        out_shape=(jax.ShapeDtypeStruct((t, OFF_GA), BF16),
                   jax.ShapeDtypeStruct((t, D_MODEL), BF16)),
        grid=grid,
        in_specs=[
            pl.BlockSpec((tm, D_MODEL), lambda i, j: (i, 0)),
            pl.BlockSpec((1, D_MODEL), lambda i, j: (0, 0)),
            pl.BlockSpec((D_MODEL, PROJ_TN), lambda i, j: (0, j)),
            pl.BlockSpec((1, HEAD_DIM), lambda i, j: (0, 0)),
            pl.BlockSpec((1, HEAD_DIM), lambda i, j: (0, 0)),
        ],
        out_specs=(pl.BlockSpec((tm, PROJ_TN), lambda i, j: (i, j)),
                   pl.BlockSpec((tm, D_MODEL), lambda i, j: (i, 0))),
        compiler_params=pltpu.CompilerParams(
            dimension_semantics=("arbitrary", "arbitrary"),
            vmem_limit_bytes=VMEM_LIMIT),
        name="in_proj",
    )(x2d, g1, w_in_bf, qg, kg)


SB_T = 256
SB_G = 4


def _softplus2(a):
    return jnp.maximum(a, 0.0) + jnp.log2(1.0 + jnp.exp2(jnp.minimum(a, -a)))


def _split_hi_lo(a):
    hi = a.astype(BF16)
    lo = (a - hi.astype(F32)).astype(BF16)
    return hi, lo


def _eye(n):
    return (lax.broadcasted_iota(jnp.int32, (n, n), 0)
            == lax.broadcasted_iota(jnp.int32, (n, n), 1)).astype(BF16)


def _fill_vt(v_ref, vt_ref, t, head_dim=HEAD_DIM, head_rows=HEAD_DIM):
    eye = _eye(HEAD_DIM)
    n_heads = v_ref.shape[2] // head_dim

    def body(c, _):
        start = pl.multiple_of(c * t, t)
        for h in range(n_heads):
            for g in range(head_dim // HEAD_DIM):
                lanes = slice(h * head_dim + g * HEAD_DIM, h * head_dim + (g + 1) * HEAD_DIM)
                rows = slice(h * head_rows + g * HEAD_DIM, h * head_rows + (g + 1) * HEAD_DIM)
                vt_ref[c, rows, :] = _dot_nt(eye, v_ref[0, pl.ds(start, t), lanes]).astype(BF16)
            if head_rows > head_dim:
                vt_ref[c, h * head_rows + head_dim:(h + 1) * head_rows, :] = jnp.ones(
                    (head_rows - head_dim, t), BF16)
        return 0

    lax.fori_loop(0, v_ref.shape[1] // t, body, 0)


SB_EXIT_LOG2 = 160.0
NORM_SLACK = 1.01


def _max_key_norm2(k_ref, kn_ref, t):
    ones8 = jnp.ones((8, HEAD_DIM), BF16)
    n_groups = k_ref.shape[2] // HEAD_DIM

    def body(c, mx):
        start = pl.multiple_of(c * t, t)
        out = []
        for g in range(n_groups):
            kf = k_ref[0, pl.ds(start, t), g * HEAD_DIM:(g + 1) * HEAD_DIM].astype(F32)
            out.append(jnp.maximum(mx[g], _dot_nt(ones8, (kf * kf).astype(BF16))))
        return tuple(out)

    mx = lax.fori_loop(0, k_ref.shape[1] // t, body,
                       tuple(jnp.zeros((8, t), F32) for _ in range(n_groups)))
    for g in range(n_groups):
        kn_ref[g:g + 1, :] = jnp.broadcast_to(
            jnp.max(mx[g][0:1, :], axis=1, keepdims=True), (1, HEAD_DIM))


GATE_CHUNKS = 8
GATE_AFTER_QK = 5
GATE_AFTER_CUM = 3


WD_CAST_EVERY = 4


def _sb_attn_kernel(q_ref, k_ref, v_ref, xn_ref, wgate_ref, wd_ref, o_ref, gate_ref, wd_out_ref,
                    vt_ref, a_ref, hl_ref, e_ref, kn_ref, gbuf_ref):
    qi = pl.program_id(2)
    t = SB_T
    heads = range(SB_G)

    @pl.when(qi % WD_CAST_EVERY == 0)
    def _():
        wd_out_ref[...] = wd_ref[...].astype(BF16)

    @pl.when(qi == 0)
    def _():
        _fill_vt(v_ref, vt_ref, t)
        _max_key_norm2(k_ref, kn_ref, t)

    r2 = lax.broadcasted_iota(jnp.int32, (t, 2 * t), 0)
    c2 = lax.broadcasted_iota(jnp.int32, (t, 2 * t), 1)
    l2 = ((c2 & (t - 1)) >= r2).astype(BF16)
    lanes = [slice(g * HEAD_DIM, (g + 1) * HEAD_DIM) for g in heads]

    def block_start(j):
        return pl.multiple_of(jnp.maximum(qi - j, 0) * t, t)

    def strict_mask():
        key = lax.broadcasted_iota(jnp.int32, (t, t), 0)
        qry = lax.broadcasted_iota(jnp.int32, (t, t), 1)
        return key < qry

    def s1_matmul(j):
        start = block_start(j)
        return [_dot_nt(k_ref[0, pl.ds(start, t), lanes[g]], q_ref[0, :, lanes[g]]) for g in heads]

    def s1_finish(a, slot, masked):
        for g in heads:
            sp = _softplus2(a[g])
            if masked:
                sp = jnp.where(strict_mask(), sp, 0.0)
            hi, lo = _split_hi_lo(sp)
            a_ref[slot, g] = a[g]
            hl_ref[slot, g, 0:t, :] = hi
            hl_ref[slot, g, t:2 * t, :] = lo

    def s2_matmul(slot):
        return [_dot(l2, hl_ref[slot, g]) for g in heads]

    def s2_finish(cum, slot, masked):
        for g in heads:
            e0 = a_ref[slot, g] - cum[g]
            if masked:
                e0 = jnp.where(strict_mask(), e0, NEG_BIG)
            e_ref[slot, g] = e0
        return tuple(c[0:1, :] for c in cum)

    def s3_weights(slot, carry):
        return [jnp.exp2(e_ref[slot, g] - carry[g]).astype(BF16) for g in heads]

    def s3_matmul(j, w):
        kb = jnp.maximum(qi - j, 0)
        return [_dot(vt_ref[kb, lanes[g], :], w[g]) for g in heads]

    def step(i, slot, st):
        acc, carry, sums = st
        w = s3_weights(slot, carry)
        a = s1_matmul(i)
        cum = s2_matmul(1 - slot)
        pv = s3_matmul(i - 2, w)
        s1_finish(a, slot, False)
        new_sums = s2_finish(cum, 1 - slot, False)
        return (tuple(acc[g] + pv[g] for g in heads),
                tuple(carry[g] + sums[g] for g in heads), new_sums)

    ones8 = jnp.ones((8, HEAD_DIM), BF16)
    bound = []
    for g in heads:
        qf = q_ref[0, :, lanes[g]].astype(F32)
        qn2 = _dot_nt(ones8, (qf * qf).astype(BF16))[0:1, :]
        bound.append(jnp.sqrt(qn2 * kn_ref[g:g + 1, 0:1]) * NORM_SLACK)

    def all_underflow(carry):
        slack = bound[0] - carry[0]
        for g in heads[1:]:
            slack = jnp.maximum(slack, bound[g] - carry[g])
        return jnp.max(slack) < -SB_EXIT_LOG2

    gate_cols = wgate_ref.shape[1] // GATE_CHUNKS

    def gate_chunk(c):
        cols = slice(c * gate_cols, (c + 1) * gate_cols)
        gbuf_ref[:, cols] = _dot(xn_ref[0], wgate_ref[:, cols])

    a0 = s1_matmul(0)
    a1 = s1_matmul(1)
    for c in range(0, GATE_AFTER_QK):
        gate_chunk(c)
    s1_finish(a0, 0, True)
    s1_finish(a1, 1, False)
    cum0 = s2_matmul(0)
    cum1 = s2_matmul(1)
    for c in range(GATE_AFTER_QK, GATE_AFTER_QK + GATE_AFTER_CUM):
        gate_chunk(c)
    sums0 = s2_finish(cum0, 0, True)
    sums1 = s2_finish(cum1, 1, False)
    w0 = s3_weights(0, tuple(jnp.zeros((1, t), F32) for _ in heads))
    w1 = s3_weights(1, sums0)
    pv0 = s3_matmul(0, w0)
    pv1 = s3_matmul(1, w1)
    for c in range(GATE_AFTER_QK + GATE_AFTER_CUM, GATE_CHUNKS):
        gate_chunk(c)
    gate_ref[0] = gbuf_ref[...].astype(gate_ref.dtype)
    acc2 = tuple(pv0[g] + jnp.where(qi > 0, pv1[g], 0.0) for g in heads)
    carry2 = tuple(sums0[g] + sums1[g] for g in heads)

    def remaining_blocks(acc, carry):
        s1_finish(s1_matmul(2), 0, False)
        a = s1_matmul(3)
        cum = s2_matmul(0)
        s1_finish(a, 1, False)
        sums = s2_finish(cum, 0, False)
        pairs = (qi - 2) // 2

        def cond(st):
            k, done = st[0], st[1]
            return (k < pairs) & jnp.logical_not(done)

        def body(st):
            k = st[0]
            acc, carry, sums = step(2 * k + 5, 1, step(2 * k + 4, 0, st[2:]))
            return (k + 1, all_underflow(carry), acc, carry, sums)

        k, _, acc, carry, sums = lax.while_loop(
            cond, body, (jnp.int32(0), False, acc, carry, sums))

        w = s3_weights(0, carry)
        cum = s2_matmul(1)
        pv = s3_matmul(2 * k + 2, w)
        s2_finish(cum, 1, False)
        acc = tuple(acc[g] + pv[g] for g in heads)
        carry = tuple(carry[g] + sums[g] for g in heads)
        pv = s3_matmul(2 * k + 3, s3_weights(1, carry))
        last_is_real = 2 * k + 3 <= qi
        return tuple(acc[g] + jnp.where(last_is_real, pv[g], 0.0) for g in heads)

    more = (qi >= 2) & jnp.logical_not(all_underflow(carry2))
    acc = lax.cond(more, lambda: remaining_blocks(acc2, carry2), lambda: acc2)
    for g in heads:
        o_ref[0, :, lanes[g]] = acc[g].T.astype(o_ref.dtype)


def _sb_attn(proj3d, xn3d, w_in_bf, wd):
    b, s, _ = proj3d.shape
    t = SB_T
    w = SB_G * HEAD_DIM
    groups = SB_HEADS // SB_G
    nq = s // t
    grid = (b, groups, nq)
    cq, ck, cv = OFF_SBQ // w, OFF_SBK // w, OFF_SBV // w
    gate_w = 2 * D_MODEL // groups
    cg = OFF_GA // gate_w
    single = pl.Buffered(1)
    slices = b * groups * nq // WD_CAST_EVERY
    wd3 = wd.reshape(slices, wd.shape[0] // slices, wd.shape[1])
    wd_spec = pl.BlockSpec(
        (1,) + wd3.shape[1:],
        lambda bi, h, qi: (((bi * groups + h) * nq + qi) // WD_CAST_EVERY, 0, 0))
    out_a, gates, wd_bf = pl.pallas_call(
        _sb_attn_kernel,
        out_shape=(jax.ShapeDtypeStruct((b, s, SB_WIDTH), BF16),
                   jax.ShapeDtypeStruct((b, s, 2 * D_MODEL), BF16),
                   jax.ShapeDtypeStruct(wd3.shape, BF16)),
        grid=grid,
        in_specs=[
            pl.BlockSpec((1, t, w), lambda bi, h, qi: (bi, qi, cq + h)),
            pl.BlockSpec((1, s, w), lambda bi, h, qi: (bi, 0, ck + h)),
            pl.BlockSpec((1, s, w), lambda bi, h, qi: (bi, 0, cv + h)),
            pl.BlockSpec((1, t, D_MODEL), lambda bi, h, qi: (bi, qi, 0)),
            pl.BlockSpec((D_MODEL, gate_w), lambda bi, h, qi: (0, cg + h)),
            wd_spec,
        ],
        out_specs=(pl.BlockSpec((1, t, w), lambda bi, h, qi: (bi, qi, h)),
                   pl.BlockSpec((1, t, gate_w), lambda bi, h, qi: (bi, qi, h)),
                   wd_spec),
        scratch_shapes=[pltpu.VMEM((s // t, w, t), BF16),
                        pltpu.VMEM((2, SB_G, t, t), F32),
                        pltpu.VMEM((2, SB_G, 2 * t, t), BF16),
                        pltpu.VMEM((2, SB_G, t, t), F32),
                        pltpu.VMEM((SB_G, HEAD_DIM), F32),
                        pltpu.VMEM((t, gate_w), F32)],
        compiler_params=pltpu.CompilerParams(
            dimension_semantics=("arbitrary", "arbitrary", "arbitrary"),
            vmem_limit_bytes=VMEM_LIMIT),
        name="sb_attn",
    )(proj3d, proj3d, proj3d, xn3d, w_in_bf, wd3)
    return out_a, gates, wd_bf.reshape(wd.shape)


DF_T = 256
DF_SLOPES = [2.0 ** (-8.0 * (h + 1) / DF_HEADS) for h in range(DF_HEADS)]
DF_V_AUG = DF_V_DIM + 16
DF_SLOPE_PIECES = 4


def _bf16_pieces(x, n):
    pieces = []
    for _ in range(n):
        p = float(np.asarray(x, dtype=np.float32).astype(jnp.bfloat16).astype(np.float32))
        pieces.append(p)
        x = x - p
    return pieces


def _df_attn_kernel(lamp_ref, q_ref, k_ref, v_ref, sg_ref, wg_ref, wu_ref,
                    o_ref, wg_out_ref, wu_out_ref,
                    bias_ref, kc_ref, qc_ref, acc_ref, vt_ref, p_ref, al_ref, m_ref):
    bi = pl.program_id(0)
    qs = pl.program_id(1)
    t = DF_T
    va = DF_V_AUG

    wg_out_ref[...] = wg_ref[...].astype(BF16)
    wu_out_ref[...] = wu_ref[...].astype(BF16)

    @pl.when(qs == 0)
    def _():
        _fill_vt(v_ref, vt_ref, t, DF_V_DIM, va)

    @pl.when((bi == 0) & (qs == 0))
    def _():
        key = lax.broadcasted_iota(jnp.int32, (t, t), 0)
        qry = lax.broadcasted_iota(jnp.int32, (t, t), 1)
        rel = (qry - key).astype(F32)
        allowed = (key // CHUNK) <= (qry // CHUNK)
        pos = lax.broadcasted_iota(jnp.int32, (t, HEAD_DIM), 0).astype(F32)
        col = lax.broadcasted_iota(jnp.int32, (t, HEAD_DIM), 1)
        n = DF_SLOPE_PIECES
        for h in range(DF_HEADS):
            sl = DF_SLOPES[h] * LOG2E
            bias_ref[h] = jnp.where(allowed, -sl * jnp.abs(rel), NEG_BIG)
            pieces = jnp.zeros((t, HEAD_DIM), F32)
            for idx, piece in enumerate(_bf16_pieces(sl, n)):
                pieces = jnp.where((col == idx) | (col == n + idx), piece, pieces)
            kc_ref[h] = jnp.where(col < n, pos, jnp.where(col < 2 * n, pieces, 0.0)).astype(BF16)
            qc_ref[h] = jnp.where(col < n, pieces, jnp.where(col < 2 * n, -pos, 0.0)).astype(BF16)

    all_chains = [(u, h, c) for u in range(2) for h in range(DF_HEADS) for c in range(2)]
    u1_chains = [ch for ch in all_chains if ch[0] == 1]

    def chain_id(ch):
        u, h, c = ch
        return (u * DF_HEADS + h) * 2 + c

    def lanes(h, c):
        return slice((2 * h + c) * HEAD_DIM, (2 * h + c + 1) * HEAD_DIM)

    def q_blk(u, h, c):
        return q_ref[0, u * t:(u + 1) * t, lanes(h, c)]

    def k_blk(kb, h, c):
        return k_ref[0, pl.ds(pl.multiple_of(kb * t, t), t), lanes(h, c)]

    def s1_offdiag(kb, chains):
        return [_dot_nt(jnp.concatenate([k_blk(kb, h, c), kc_ref[h]], axis=1),
                        jnp.concatenate([q_blk(u, h, c), qc_ref[h]], axis=1))
                for u, h, c in chains]

    def s1_diag():
        return [_dot_nt(k_blk(2 * qs + u, h, c), q_blk(u, h, c)) + bias_ref[h]
                for u, h, c in all_chains]

    def s1_finish(s, chains, blocks_away, slot):
        for s_i, ch in zip(s, chains):
            i = chain_id(ch)
            row = slice(i, i + 1)
            shift = -(DF_SLOPES[ch[1]] * LOG2E) * (blocks_away(ch[0]) * t).astype(F32)
            m = m_ref[row, :]
            m_new = jnp.maximum(m, jnp.max(s_i, axis=0, keepdims=True) + shift)
            m_ref[row, :] = m_new
            al_ref[slot, row, :] = jnp.exp2(m - m_new)
            p_ref[slot, i] = jnp.exp2(s_i - (m_new - shift)).astype(BF16)

    def s2_matmul(chains, key_block, slot):
        return [_dot(vt_ref[key_block(u), h * va:(h + 1) * va, :], p_ref[slot, chain_id((u, h, c))])
                for u, h, c in chains]

    def s2_finish(pv, chains, slot):
        for pv_i, ch in zip(pv, chains):
            i = chain_id(ch)
            u, h, c = ch
            acc_ref[u, h, c] = al_ref[slot, i:i + 1, :] * acc_ref[u, h, c] + pv_i

    acc_ref[...] = jnp.zeros_like(acc_ref)
    m_ref[...] = jnp.full_like(m_ref, NEG_BIG)

    s1_finish(s1_offdiag(2 * qs, u1_chains), u1_chains, lambda u: jnp.int32(1), 0)

    s = s1_diag()
    pv = s2_matmul(u1_chains, lambda u: 2 * qs, 0)
    s2_finish(pv, u1_chains, 0)
    s1_finish(s, all_chains, lambda u: jnp.int32(0), 1)

    def common_step(kb, slot, prev_block):
        s = s1_offdiag(kb, all_chains)
        pv = s2_matmul(all_chains, prev_block, 1 - slot)
        s2_finish(pv, all_chains, 1 - slot)
        s1_finish(s, all_chains, lambda u: 2 * qs + u - kb, slot)

    def body(k, carry):
        common_step(2 * k, 0, lambda u: jnp.where(k == 0, 2 * qs + u, 2 * k - 1))
        common_step(2 * k + 1, 1, lambda u: 2 * k)
        return carry

    lax.fori_loop(0, qs, body, 0)

    pv = s2_matmul(all_chains, lambda u: jnp.where(qs == 0, 2 * qs + u, 2 * qs - 1), 1)
    s2_finish(pv, all_chains, 1)

    lp = lamp_ref[...]
    lam = (jnp.exp(jnp.sum(lp[0:1] * lp[1:2], axis=-1, keepdims=True))
           - jnp.exp(jnp.sum(lp[2:3] * lp[3:4], axis=-1, keepdims=True)) + LAMBDA_INIT)
    for u in range(2):
        for h in range(DF_HEADS):
            a1, a2 = acc_ref[u, h, 0], acc_ref[u, h, 1]
            l1, l2 = a1[DF_V_DIM:DF_V_DIM + 1, :], a2[DF_V_DIM:DF_V_DIM + 1, :]
            out_t = a1[0:DF_V_DIM, :] / l1 - lam * (a2[0:DF_V_DIM, :] / l2)
            ms = jnp.mean(out_t * out_t, axis=0, keepdims=True)
            out = (out_t * lax.rsqrt(ms + SUBLN_EPS)).T * (sg_ref[...] * (1.0 - LAMBDA_INIT))
            o_ref[0, u * t:(u + 1) * t, h * DF_V_DIM:(h + 1) * DF_V_DIM] = out.astype(o_ref.dtype)


def _df_attn(proj3d, lam_params, subln_g, wg, wu):
    b, s, _ = proj3d.shape
    t = DF_T
    tq = 2 * t
    nq = s // tq
    grid = (b, nq)
    steps = b * nq
    cq, ck, cv = OFF_DFQ // DF_QK_WIDTH, OFF_DFK // DF_QK_WIDTH, OFF_DFV // DF_V_WIDTH
    single = pl.Buffered(1)
    d, d_ff = wg.shape
    wg3 = wg.reshape(steps, d // steps, d_ff)
    wu3 = wu.reshape(steps, d // steps, d_ff)
    gu_spec = pl.BlockSpec((1,) + wg3.shape[1:], lambda bi, qs: (bi * nq + qs, 0, 0))
    out_b, wg_bf, wu_bf = pl.pallas_call(
        _df_attn_kernel,
        out_shape=(jax.ShapeDtypeStruct((b, s, DF_V_WIDTH), BF16),
                   jax.ShapeDtypeStruct(wg3.shape, BF16),
                   jax.ShapeDtypeStruct(wu3.shape, BF16)),
        grid=grid,
        in_specs=[
            pl.BlockSpec((4, HEAD_DIM), lambda bi, qs: (0, 0)),
            pl.BlockSpec((1, tq, DF_QK_WIDTH), lambda bi, qs: (bi, qs, cq)),
            pl.BlockSpec((1, s, DF_QK_WIDTH), lambda bi, qs: (bi, 0, ck), pipeline_mode=single),
            pl.BlockSpec((1, s, DF_V_WIDTH), lambda bi, qs: (bi, 0, cv), pipeline_mode=single),
            pl.BlockSpec((1, DF_V_DIM), lambda bi, qs: (0, 0)),
            gu_spec, gu_spec,
        ],
        out_specs=(pl.BlockSpec((1, tq, DF_V_WIDTH), lambda bi, qs: (bi, qs, 0)),
                   gu_spec, gu_spec),
        scratch_shapes=[pltpu.VMEM((DF_HEADS, t, t), F32),
                        pltpu.VMEM((DF_HEADS, t, HEAD_DIM), BF16),
                        pltpu.VMEM((DF_HEADS, t, HEAD_DIM), BF16),
                        pltpu.VMEM((2, DF_HEADS, 2, DF_V_AUG, t), F32),
                        pltpu.VMEM((s // t, DF_HEADS * DF_V_AUG, t), BF16),
                        pltpu.VMEM((2, 4 * DF_HEADS, t, t), BF16),
                        pltpu.VMEM((2, 4 * DF_HEADS, t), F32),
                        pltpu.VMEM((4 * DF_HEADS, t), F32)],
        compiler_params=pltpu.CompilerParams(
            dimension_semantics=("arbitrary", "arbitrary"),
            vmem_limit_bytes=VMEM_LIMIT),
        name="df_attn",
    )(lam_params, proj3d, proj3d, proj3d, subln_g, wg3, wu3)
    return out_b, wg_bf.reshape(wg.shape), wu_bf.reshape(wu.shape)


def _merge_kernel(a_ref, wa_ref, b_ref, wb_ref, ga_ref, gb_ref, o_ref):
    pa = _dot(a_ref[...], wa_ref[...])
    pb = _dot(b_ref[...], wb_ref[...])
    ga = jax.nn.sigmoid(ga_ref[...].astype(F32))
    gb = jax.nn.sigmoid(gb_ref[...].astype(F32))
    o_ref[...] = (ga * pa + gb * pb).astype(o_ref.dtype)


def _merge(out_a, wa, out_b, wb, gates, tm=512, tn=D_MODEL):
    t = out_a.shape[0]
    grid = (t // tm, D_MODEL // tn)
    ca, cb = 0, D_MODEL // tn
    return pl.pallas_call(
        _merge_kernel,
        out_shape=jax.ShapeDtypeStruct((t, D_MODEL), BF16),
        grid=grid,
        in_specs=[
            pl.BlockSpec((tm, SB_WIDTH), lambda i, j: (i, 0)),
            pl.BlockSpec((SB_WIDTH, tn), lambda i, j: (0, j)),
            pl.BlockSpec((tm, DF_V_WIDTH), lambda i, j: (i, 0)),
            pl.BlockSpec((DF_V_WIDTH, tn), lambda i, j: (0, j)),
            pl.BlockSpec((tm, tn), lambda i, j: (i, ca + j)),
            pl.BlockSpec((tm, tn), lambda i, j: (i, cb + j)),
        ],
        out_specs=pl.BlockSpec((tm, tn), lambda i, j: (i, j)),
        compiler_params=pltpu.CompilerParams(
            dimension_semantics=("arbitrary", "arbitrary"),
            vmem_limit_bytes=VMEM_LIMIT),
        name="merge",
    )(out_a, wa, out_b, wb, gates, gates)


def _out_proj_kernel(m_ref, w_ref, x_ref, g_ref, h_ref, hn_ref):
    h = x_ref[...] + _dot(m_ref[...], w_ref[...])
    h_ref[...] = h
    ms = jnp.mean(h * h, axis=-1, keepdims=True)
    hn_ref[...] = (h * lax.rsqrt(ms + EPS) * g_ref[...]).astype(hn_ref.dtype)


def _out_proj(merged, w_out, x2d, g2, tm=512):
    t = merged.shape[0]
    grid = (t // tm,)
    return pl.pallas_call(
        _out_proj_kernel,
        out_shape=(jax.ShapeDtypeStruct((t, D_MODEL), F32),
                   jax.ShapeDtypeStruct((t, D_MODEL), BF16)),
        grid=grid,
        in_specs=[
            pl.BlockSpec((tm, D_MODEL), lambda i: (i, 0)),
            pl.BlockSpec((D_MODEL, D_MODEL), lambda i: (0, 0)),
            pl.BlockSpec((tm, D_MODEL), lambda i: (i, 0)),
            pl.BlockSpec((1, D_MODEL), lambda i: (0, 0)),
        ],
        out_specs=(pl.BlockSpec((tm, D_MODEL), lambda i: (i, 0)),
                   pl.BlockSpec((tm, D_MODEL), lambda i: (i, 0))),
        compiler_params=pltpu.CompilerParams(
            dimension_semantics=("arbitrary",),
            vmem_limit_bytes=VMEM_LIMIT),
        name="out_proj",
    )(merged, w_out, x2d, g2)


FFN_SUB = 2


def _ffn_kernel(hn_ref, h_ref, wg_ref, wu_ref, wd_ref, o_ref):
    j = pl.program_id(1)

    @pl.when(j == 0)
    def _():
        o_ref[...] = h_ref[...]

    hn = hn_ref[...]
    tf = wg_ref.shape[1]
    sub = tf // FFN_SUB
    gu = []
    for c in range(FFN_SUB):
        cols = slice(c * sub, (c + 1) * sub)
        gu.append((_dot(hn, wg_ref[:, cols]), _dot(hn, wu_ref[:, cols])))
    acc = None
    for c in range(FFN_SUB):
        g, u = gu[c]
        hid = (g * jax.nn.sigmoid(g) * u).astype(BF16)
        d = _dot(hid, wd_ref[c * sub:(c + 1) * sub, :])
        acc = d if acc is None else acc + d
    o_ref[...] += acc


def _ffn(hn, h, wg, wu, wd, tm=512, tf=512):
    t = hn.shape[0]
    d_ff = wg.shape[1]
    grid = (t // tm, d_ff // tf)
    return pl.pallas_call(
        _ffn_kernel,
        out_shape=jax.ShapeDtypeStruct((t, D_MODEL), F32),
        grid=grid,
        in_specs=[
            pl.BlockSpec((tm, D_MODEL), lambda i, j: (i, 0)),
            pl.BlockSpec((tm, D_MODEL), lambda i, j: (i, 0)),
            pl.BlockSpec((D_MODEL, tf), lambda i, j: (0, j)),
            pl.BlockSpec((D_MODEL, tf), lambda i, j: (0, j)),
            pl.BlockSpec((tf, D_MODEL), lambda i, j: (j, 0)),
        ],
        out_specs=pl.BlockSpec((tm, D_MODEL), lambda i, j: (i, 0)),
        compiler_params=pltpu.CompilerParams(
            dimension_semantics=("arbitrary", "arbitrary"),
            vmem_limit_bytes=VMEM_LIMIT),
        name="ffn",
    )(hn, h, wg, wu, wd)


def kernel(x, norm1_g, w_in, q_norm_g, k_norm_g, lambda_q1, lambda_k1, lambda_q2, lambda_k2,
           subln_g, w_branch_a, w_branch_b, w_out, norm2_g, w_ffn_gate, w_ffn_up, w_ffn_down):
    b, s, d = x.shape
    t = b * s
    layer = 0
    x2d = x.reshape(t, d)
    lam_params = jnp.stack([lambda_q1[layer], lambda_k1[layer], lambda_q2[layer], lambda_k2[layer]])

    w_in_bf = w_in[layer].astype(BF16)
    proj, xn = _in_proj(x2d, norm1_g[layer].reshape(1, d), w_in_bf,
                        q_norm_g[layer].reshape(1, HEAD_DIM), k_norm_g[layer].reshape(1, HEAD_DIM))
    proj3d = proj.reshape(b, s, OFF_GA)

    out_a, gates, wd = _sb_attn(proj3d, xn.reshape(b, s, d), w_in_bf, w_ffn_down[layer])
    out_a = out_a.reshape(t, SB_WIDTH)
    out_b, wg, wu = _df_attn(proj3d, lam_params, subln_g[layer].reshape(1, DF_V_DIM),
                             w_ffn_gate[layer], w_ffn_up[layer])
    out_b = out_b.reshape(t, DF_V_WIDTH)

    merged = _merge(out_a, w_branch_a[layer].astype(BF16), out_b, w_branch_b[layer].astype(BF16),
                    gates.reshape(t, 2 * d))
    h, hn = _out_proj(merged, w_out[layer].astype(BF16), x2d, norm2_g[layer].reshape(1, d))
    out = _ffn(hn, h, wg, wu, wd)
    return out.reshape(b, s, d)
```
